```python
import jax, jax.numpy as jnp
from jax import lax
import numpy as np

D_MODEL = 1024
BATCH = 2
SEQ = 8192
DEPTH = 4

HEAD_DIM = 64
ROPE_THETA = 10000.0
RMS_EPS = 1e-6
QBLK = 128
NEG_INF = -1e30

GRID_W = 64
NA_HEADS = 8
NA_ROWS = 8
NA_COLS = 16
NA_WIDTH = NA_HEADS * HEAD_DIM

MLA_HEADS = 8
MLA_Q_RANK = 256
MLA_KV_RANK = 128
MLA_NOPE = 64
MLA_ROPE = 32
MLA_V = 64

DIL_PAIRS = ((128, 1), (512, 4), (2048, 16))
DIL_GROUPS = len(DIL_PAIRS)
DIL_HEADS = D_MODEL // HEAD_DIM

D_FF = 4 * D_MODEL
N_EVEN = (DEPTH + 1) // 2
N_ODD = DEPTH // 2

EVEN_IN = 3 * NA_WIDTH + MLA_Q_RANK + MLA_KV_RANK + MLA_ROPE
EVEN_MIX = NA_WIDTH + MLA_HEADS * MLA_V
ODD_IN = DIL_GROUPS * 3 * DIL_HEADS * HEAD_DIM
ODD_MIX = DIL_HEADS * HEAD_DIM

kernel_name = "hybrid_na_mla_dilated_encoder"


def rms_norm(x, g):
    xf = x.astype(jnp.float32)
    y = xf * lax.rsqrt(jnp.mean(xf * xf, axis=-1, keepdims=True) + RMS_EPS)
    return (y * g.astype(jnp.float32)).astype(x.dtype)


def rope_tables(seq_len, dim):
    inv = 1.0 / (ROPE_THETA ** (jnp.arange(0, dim, 2, dtype=jnp.float32) / dim))
    ang = jnp.arange(seq_len, dtype=jnp.float32)[:, None] * inv[None, :]
    return jnp.cos(ang), jnp.sin(ang)


def apply_rope(x, cos, sin):
    half = x.shape[-1] // 2
    shp = (1, cos.shape[0]) + (1,) * (x.ndim - 3) + (cos.shape[1],)
    c = cos.reshape(shp).astype(x.dtype)
    s = sin.reshape(shp).astype(x.dtype)
    x1, x2 = x[..., :half], x[..., half:]
    return jnp.concatenate([x1 * c - x2 * s, x2 * c + x1 * s], axis=-1)


def neighbourhood_attention(q, k, v, rpb):
    b, s, h, dh = q.shape
    rows = s // GRID_W
    wr = min(NA_ROWS, rows)
    wc = NA_COLS
    qg = q.reshape(b, rows, GRID_W, h, dh)
    kg = k.reshape(b, rows, GRID_W, h, dh)
    vg = v.reshape(b, rows, GRID_W, h, dh)
    cols = jnp.arange(GRID_W)
    col_start = jnp.clip(cols - wc // 2, 0, GRID_W - wc)
    col_idx = col_start[:, None] + jnp.arange(wc)[None, :]
    col_rel = col_idx - cols[:, None] + (NA_COLS - 1)
    scale = dh ** -0.5

    def row_fn(r):
        r_start = jnp.clip(r - wr // 2, 0, rows - wr)
        row_rel = r_start + jnp.arange(wr) - r + (NA_ROWS - 1)
        q_row = lax.dynamic_index_in_dim(qg, r, axis=1, keepdims=False)
        k_rows = lax.dynamic_slice_in_dim(kg, r_start, wr, axis=1)
        v_rows = lax.dynamic_slice_in_dim(vg, r_start, wr, axis=1)
        k_win = k_rows[:, :, col_idx]
        v_win = v_rows[:, :, col_idx]
        sc = jnp.einsum('bchd,bicjhd->bhcij', q_row, k_win,
                        preferred_element_type=jnp.float32) * scale
        bias = rpb[:, row_rel[:, None, None], col_rel[None, :, :]]
        sc = sc + bias.transpose(0, 2, 1, 3).astype(jnp.float32)[None]
        p = jax.nn.softmax(sc.reshape(b, h, GRID_W, wr * wc), axis=-1).reshape(sc.shape)
        return jnp.einsum('bhcij,bicjhd->bchd', p.astype(v.dtype), v_win)

    out = lax.map(row_fn, jnp.arange(rows))
    return out.transpose(1, 0, 2, 3, 4).reshape(b, s, h * dh)


def dense_attention(q, k, v, scale):
    b, s, h, dq = q.shape
    nblk = s // QBLK
    qb = q.reshape(b, nblk, QBLK, h, dq).transpose(1, 0, 2, 3, 4)

    def blk(qi):
        sc = jnp.einsum('bqhd,bkhd->bhqk', qi, k, preferred_element_type=jnp.float32) * scale
        p = jax.nn.softmax(sc, axis=-1)
        return jnp.einsum('bhqk,bkhd->bqhd', p.astype(v.dtype), v)

    out = lax.map(blk, qb)
    return out.transpose(1, 0, 2, 3, 4).reshape(b, s, h * v.shape[-1])


def dilated_attention(q, k, v):
    b, s, g, h, dh = q.shape
    nblk = s // QBLK
    scale = dh ** -0.5
    offsets = [jnp.arange(-(w // 2), w // 2 + 1, d) for (w, d) in DIL_PAIRS]
    ks = [k[:, :, gi] for gi in range(g)]
    vs = [v[:, :, gi] for gi in range(g)]

    def blk(bi):
        start = bi * QBLK
        qpos = start + jnp.arange(QBLK)
        q_blk = lax.dynamic_slice_in_dim(q, start, QBLK, axis=1)
        outs, lses = [], []
        for gi in range(g):
            kpos = qpos[:, None] + offsets[gi][None, :]
            valid = (kpos >= 0) & (kpos < s)
            kidx = jnp.clip(kpos, 0, s - 1)
            k_sel = ks[gi][:, kidx]
            v_sel = vs[gi][:, kidx]
            sc = jnp.einsum('bqhd,bqjhd->bhqj', q_blk[:, :, gi], k_sel,
                            preferred_element_type=jnp.float32) * scale
            sc = jnp.where(valid[None, None], sc, NEG_INF)
            m = jnp.max(sc, axis=-1, keepdims=True)
            e = jnp.exp(sc - m)
            z = jnp.sum(e, axis=-1, keepdims=True)
            o = jnp.einsum('bhqj,bqjhd->bqhd', (e / z).astype(v.dtype), v_sel)
            outs.append(o.astype(jnp.float32))
            lses.append((m + jnp.log(z))[..., 0])
        lse = jnp.stack(lses, axis=0)
        wgt = jax.nn.softmax(lse, axis=0).transpose(0, 1, 3, 2)[..., None]
        out = jnp.sum(wgt * jnp.stack(outs, axis=0), axis=0)
        return out.astype(v.dtype)

    out = lax.map(blk, jnp.arange(nblk))
    return out.transpose(1, 0, 2, 3, 4).reshape(b, s, h * dh)


def even_mixer(xn, w_in, rpb, q_norm, w_uq, kv_norm, w_ukv, w_o, cos_r, sin_r):
    b, s, _ = xn.shape
    hcat = xn @ w_in
    o0 = 3 * NA_WIDTH
    o1 = o0 + MLA_Q_RANK
    o2 = o1 + MLA_KV_RANK
    a_qkv = hcat[..., :o0].reshape(b, s, 3, NA_HEADS, HEAD_DIM)
    out_a = neighbourhood_attention(a_qkv[:, :, 0], a_qkv[:, :, 1], a_qkv[:, :, 2], rpb)
    c_q = rms_norm(hcat[..., o0:o1], q_norm)
    c_kv = rms_norm(hcat[..., o1:o2], kv_norm)
    k_pe = apply_rope(hcat[..., o2:][:, :, None, :], cos_r, sin_r)
    q = (c_q @ w_uq).reshape(b, s, MLA_HEADS, MLA_NOPE + MLA_ROPE)
    q = jnp.concatenate([q[..., :MLA_NOPE], apply_rope(q[..., MLA_NOPE:], cos_r, sin_r)], axis=-1)
    kv = (c_kv @ w_ukv).reshape(b, s, MLA_HEADS, MLA_NOPE + MLA_V)
    k = jnp.concatenate([kv[..., :MLA_NOPE],
                         jnp.broadcast_to(k_pe, (b, s, MLA_HEADS, MLA_ROPE))], axis=-1)
    v = kv[..., MLA_NOPE:]
    out_b = dense_attention(q, k, v, (MLA_NOPE + MLA_ROPE) ** -0.5)
    return jnp.concatenate([out_a, out_b], axis=-1) @ w_o


def odd_mixer(xn, w_in, w_o, cos_f, sin_f):
    b, s, _ = xn.shape
    hcat = (xn @ w_in).reshape(b, s, DIL_GROUPS, 3, DIL_HEADS, HEAD_DIM)
    q = apply_rope(hcat[:, :, :, 0], cos_f, sin_f)
    k = apply_rope(hcat[:, :, :, 1], cos_f, sin_f)
    v = hcat[:, :, :, 2]
    return dilated_attention(q, k, v) @ w_o


def sq_relu_mlp(xn, w1, w2):
    return jnp.square(jax.nn.relu(xn @ w1)) @ w2


def setup_inputs(seed: int = 0) -> dict:
    key = jax.random.key(seed)
    ks = jax.random.split(key, 16)
    f32 = jnp.float32

    def nrm(k, shape, scale):
        return jax.random.normal(k, shape, f32) * scale

    return {
        "x": nrm(ks[0], (BATCH, SEQ, D_MODEL), 1.0),
        "norm_mix": 1.0 + nrm(ks[1], (DEPTH, D_MODEL), 0.1),
        "norm_mlp": 1.0 + nrm(ks[2], (DEPTH, D_MODEL), 0.1),
        "norm_final": 1.0 + nrm(ks[3], (D_MODEL,), 0.1),
        "ev_w_in": nrm(ks[4], (N_EVEN, D_MODEL, EVEN_IN), D_MODEL ** -0.5),
        "ev_rpb": nrm(ks[5], (N_EVEN, NA_HEADS, 2 * NA_ROWS - 1, 2 * NA_COLS - 1), 0.5),
        "ev_q_norm": 1.0 + nrm(ks[6], (N_EVEN, MLA_Q_RANK), 0.1),
        "ev_w_uq": nrm(ks[7], (N_EVEN, MLA_Q_RANK, MLA_HEADS * (MLA_NOPE + MLA_ROPE)), MLA_Q_RANK ** -0.5),
        "ev_kv_norm": 1.0 + nrm(ks[8], (N_EVEN, MLA_KV_RANK), 0.1),
        "ev_w_ukv": nrm(ks[9], (N_EVEN, MLA_KV_RANK, MLA_HEADS * (MLA_NOPE + MLA_V)), MLA_KV_RANK ** -0.5),
        "ev_w_o": nrm(ks[10], (N_EVEN, EVEN_MIX, D_MODEL), EVEN_MIX ** -0.5),
        "od_w_in": nrm(ks[11], (N_ODD, D_MODEL, ODD_IN), D_MODEL ** -0.5),
        "od_w_o": nrm(ks[12], (N_ODD, ODD_MIX, D_MODEL), ODD_MIX ** -0.5),
        "mlp_w1": nrm(ks[13], (DEPTH, D_MODEL, D_FF), D_MODEL ** -0.5),
        "mlp_w2": nrm(ks[14], (DEPTH, D_FF, D_MODEL), D_FF ** -0.5),
    }


def reference(x, norm_mix, norm_mlp, norm_final, ev_w_in, ev_rpb, ev_q_norm, ev_w_uq,
              ev_kv_norm, ev_w_ukv, ev_w_o, od_w_in, od_w_o, mlp_w1, mlp_w2):
    s = x.shape[1]
    cos_r, sin_r = rope_tables(s, MLA_ROPE)
    cos_f, sin_f = rope_tables(s, HEAD_DIM)
    for layer in range(DEPTH):
        xn = rms_norm(x, norm_mix[layer])
        if layer % 2 == 0:
            e = layer // 2
            x = x + even_mixer(xn, ev_w_in[e], ev_rpb[e], ev_q_norm[e], ev_w_uq[e],
                               ev_kv_norm[e], ev_w_ukv[e], ev_w_o[e], cos_r, sin_r)
        else:
            o = layer // 2
            x = x + odd_mixer(xn, od_w_in[o], od_w_o[o], cos_f, sin_f)
        x = x + sq_relu_mlp(rms_norm(x, norm_mlp[layer]), mlp_w1[layer], mlp_w2[layer])
    return rms_norm(x, norm_final)
```

```python
import functools
import math

import numpy as np
import jax
import jax.numpy as jnp
from jax import lax
from jax.experimental import pallas as pl
from jax.experimental.pallas import tpu as pltpu

F32 = jnp.float32
BF16 = jnp.bfloat16

D_MODEL = 1024
HEAD_DIM = 64
ROPE_THETA = 10000.0
RMS_EPS = 1e-6
NEG_INF = -1e30
LOG2E = math.log2(math.e)

GRID_W = 64
NA_HEADS = 8
NA_ROWS = 8
NA_COLS = 16
NA_WIDTH = NA_HEADS * HEAD_DIM

MLA_HEADS = 8
MLA_Q_RANK = 256
MLA_KV_RANK = 128
MLA_NOPE = 64
MLA_ROPE = 32
MLA_V = 64
MLA_PAD = 128

DIL_PAIRS = ((128, 1), (512, 4), (2048, 16))
DIL_HEADS = D_MODEL // HEAD_DIM
DIL_HALF = 64
DIL_QBLK = 128
DIL_KBLK = 256

D_FF = 4 * D_MODEL
EVEN_IN_PAD = 2048

LANES = 128
VMEM_LIMIT_BYTES = 56 * 1024 * 1024


def _params(*semantics):
    return pltpu.CompilerParams(dimension_semantics=semantics,
                                vmem_limit_bytes=VMEM_LIMIT_BYTES)


def _rms(x, g):
    ms = jnp.mean(x * x, axis=-1, keepdims=True)
    return x * lax.rsqrt(ms + RMS_EPS) * g


def _lane_iota(shape):
    return lax.broadcasted_iota(jnp.int32, shape, len(shape) - 1)


def _rope_chunk(x, cos, sin_signed, half):
    lane = _lane_iota(x.shape)
    fwd = pltpu.roll(x, LANES - half, 1)
    bwd = pltpu.roll(x, half, 1)
    rot = jnp.where((lane % (2 * half)) < half, fwd, bwd)
    return x * cos + rot * sin_signed


def _norm_matmul_kernel(x_ref, g_ref, w_ref, o_ref, xn_ref):
    @pl.when(pl.program_id(1) == 0)
    def _():
        xn_ref[...] = _rms(x_ref[...], g_ref[...]).astype(BF16)

    o_ref[...] = jnp.dot(xn_ref[...], w_ref[...],
                         preferred_element_type=F32).astype(o_ref.dtype)


def _norm_matmul_rope_kernel(x_ref, g_ref, w_ref, cos_ref, sin_ref, o_ref, xn_ref,
                             *, q_scale):
    j = pl.program_id(1)

    @pl.when(j == 0)
    def _():
        xn_ref[...] = _rms(x_ref[...], g_ref[...]).astype(BF16)

    acc = jnp.dot(xn_ref[...], w_ref[...], preferred_element_type=F32)
    kind = j % 3

    def roped(scale):
        cos = cos_ref[...]
        sin = sin_ref[...]
        for c in range(acc.shape[1] // LANES):
            sl = slice(c * LANES, (c + 1) * LANES)
            y = _rope_chunk(acc[:, sl], cos, sin, HEAD_DIM // 2)
            if scale != 1.0:
                y = y * scale
            o_ref[:, sl] = y.astype(o_ref.dtype)

    @pl.when(kind == 0)
    def _():
        roped(q_scale)

    @pl.when(kind == 1)
    def _():
        roped(1.0)

    @pl.when(kind == 2)
    def _():
        o_ref[...] = acc.astype(o_ref.dtype)


def norm_matmul(x, g, w, *, tm=512, tn=1024, rope=None, q_scale=1.0, seq=None):
    n, d = x.shape
    f = w.shape[1]
    tn = min(tn, f)
    grid = (n // tm, f // tn)
    in_specs = [
        pl.BlockSpec((tm, d), lambda i, j: (i, 0)),
        pl.BlockSpec((1, d), lambda i, j: (0, 0)),
        pl.BlockSpec((d, tn), lambda i, j: (0, j)),
    ]
    args = [x, g.reshape(1, d), w]
    if rope is None:
        body = _norm_matmul_kernel
    else:
        assert tn == DIL_HEADS * HEAD_DIM
        nsb = seq // tm
        in_specs += [pl.BlockSpec((tm, LANES), lambda i, j: (i % nsb, 0))] * 2
        args += list(rope)
        body = functools.partial(_norm_matmul_rope_kernel, q_scale=q_scale)
    return pl.pallas_call(
        body,
        out_shape=jax.ShapeDtypeStruct((n, f), BF16),
        grid=grid,
        in_specs=in_specs,
        out_specs=pl.BlockSpec((tm, tn), lambda i, j: (i, j)),
        scratch_shapes=[pltpu.VMEM((tm, d), BF16)],
        compiler_params=_params("parallel", "arbitrary"),
        name="norm_matmul" if rope is None else "norm_matmul_rope",
    )(*args)


def _na_kernel(q_ref, k_ref, v_ref, b_ref, o_ref, *, rows):
    wr = NA_ROWS
    kw = wr * GRID_W
    scale = HEAD_DIM ** -0.5

    def row(r, carry):
        r_start = jnp.clip(r - wr // 2, 0, rows - wr)
        didx = r - r_start
        qoff = pl.multiple_of(r * GRID_W, GRID_W)
        koff = pl.multiple_of(r_start * GRID_W, GRID_W)
        outs = []
        for hh in range(2):
            hs = slice(hh * HEAD_DIM, (hh + 1) * HEAD_DIM)
            q = q_ref[pl.ds(qoff, GRID_W), hs]
            k = k_ref[pl.ds(koff, kw), hs]
            v = v_ref[pl.ds(koff, kw), hs]
            s = lax.dot_general(q, k, (((1,), (1,)), ((), ())),
                                preferred_element_type=F32)
            s = s * scale + b_ref[hh, didx]
            m = jnp.max(s, axis=-1, keepdims=True)
            e = jnp.exp(s - m)
            z = jnp.sum(e, axis=-1, keepdims=True)
            o = jnp.dot(e.astype(BF16), v, preferred_element_type=F32)
            outs.append(o / z)
        o_ref[pl.ds(qoff, GRID_W), :] = jnp.concatenate(outs, axis=-1).astype(o_ref.dtype)
        return carry

    lax.fori_loop(0, rows, row, 0)


def _na_bias_table(rpb, rows):
    wr = min(NA_ROWS, rows)
    assert wr == NA_ROWS
    cols = np.arange(GRID_W)
    col_start = np.clip(cols - NA_COLS // 2, 0, GRID_W - NA_COLS)
    kc = np.arange(GRID_W)
    inside = (kc[None, :] >= col_start[:, None]) & (kc[None, :] < col_start[:, None] + NA_COLS)
    col_rel = np.clip(kc[None, :] - cols[:, None] + (NA_COLS - 1), 0, 2 * NA_COLS - 2)
    delta = np.arange(NA_ROWS)
    i = np.arange(wr)
    row_rel = np.clip(i[None, :] - delta[:, None] + (NA_ROWS - 1), 0, 2 * NA_ROWS - 2)
    b = rpb[:, row_rel[:, :, None, None], col_rel[None, None, :, :]]
    b = jnp.where(jnp.asarray(inside)[None, None, None], b.astype(F32), NEG_INF)
    b = b.transpose(0, 1, 3, 2, 4)
    return b.reshape(rpb.shape[0], NA_ROWS, GRID_W, wr * GRID_W)


def na_attention(hcat, bias, *, batch, seq):
    n = hcat.shape[0]
    rows = seq // GRID_W
    pairs = NA_HEADS // 2
    blk = (seq, LANES)
    return pl.pallas_call(
        functools.partial(_na_kernel, rows=rows),
        out_shape=jax.ShapeDtypeStruct((n, NA_WIDTH), BF16),
        grid=(batch, pairs),
        in_specs=[
            pl.BlockSpec(blk, lambda b, p: (b, p)),
            pl.BlockSpec(blk, lambda b, p: (b, pairs + p)),
            pl.BlockSpec(blk, lambda b, p: (b, 2 * pairs + p)),
            pl.BlockSpec((2, NA_ROWS, GRID_W, NA_ROWS * GRID_W), lambda b, p: (p, 0, 0, 0)),
        ],
        out_specs=pl.BlockSpec(blk, lambda b, p: (b, p)),
        compiler_params=_params("parallel", "parallel"),
        name="na_attention",
    )(hcat, hcat, hcat, bias)


def _mla_prep_kernel(h_ref, qn_ref, kvn_ref, wq_ref, wk_ref, wv_ref, e_ref,
                     cos_ref, sin_ref, q_ref, k_ref, v_ref, *, q_scale):
    h = h_ref[...].astype(F32)
    o1 = MLA_Q_RANK
    o2 = o1 + MLA_KV_RANK
    c_q = _rms(h[:, :o1], qn_ref[...]).astype(BF16)
    c_kv = _rms(h[:, o1:o2], kvn_ref[...]).astype(BF16)
    k_pe = h_ref[:, o2:o2 + MLA_ROPE]
    q = jnp.dot(c_q, wq_ref[...], preferred_element_type=F32)
    k = (jnp.dot(c_kv, wk_ref[...], preferred_element_type=F32)
         + jnp.dot(k_pe, e_ref[...], preferred_element_type=F32))
    v_ref[...] = jnp.dot(c_kv, wv_ref[...], preferred_element_type=F32).astype(v_ref.dtype)
    cos = cos_ref[...]
    sin = sin_ref[...]
    for hd in range(MLA_HEADS):
        sl = slice(hd * MLA_PAD, (hd + 1) * MLA_PAD)
        q_ref[:, sl] = (_rope_chunk(q[:, sl], cos, sin, MLA_ROPE // 2) * q_scale).astype(q_ref.dtype)
        k_ref[:, sl] = _rope_chunk(k[:, sl], cos, sin, MLA_ROPE // 2).astype(k_ref.dtype)


def mla_prep(hcat, q_norm, w_uq, kv_norm, w_ukv, cos_t, sin_t, *, seq, q_scale, tm=512):
    n = hcat.shape[0]
    hq = MLA_NOPE + MLA_ROPE
    wq = w_uq.reshape(MLA_Q_RANK, MLA_HEADS, hq)
    wq = jnp.pad(wq, ((0, 0), (0, 0), (0, MLA_PAD - hq))).reshape(MLA_Q_RANK, MLA_HEADS * MLA_PAD)
    wkv = w_ukv.reshape(MLA_KV_RANK, MLA_HEADS, MLA_NOPE + MLA_V)
    wk = jnp.pad(wkv[:, :, :MLA_NOPE], ((0, 0), (0, 0), (0, MLA_PAD - MLA_NOPE)))
    wk = wk.reshape(MLA_KV_RANK, MLA_HEADS * MLA_PAD)
    wv = wkv[:, :, MLA_NOPE:].reshape(MLA_KV_RANK, MLA_HEADS * MLA_V)
    e = np.zeros((MLA_ROPE, MLA_HEADS, MLA_PAD), np.float32)
    for t in range(MLA_ROPE):
        e[t, :, MLA_NOPE + t] = 1.0
    e = jnp.asarray(e.reshape(MLA_ROPE, MLA_HEADS * MLA_PAD), BF16)
    col_blk = (3 * NA_WIDTH) // 512
    nsb = seq // tm
    full = lambda shape: pl.BlockSpec(shape, lambda i: (0,) * len(shape))
    return pl.pallas_call(
        functools.partial(_mla_prep_kernel, q_scale=q_scale),
        out_shape=(jax.ShapeDtypeStruct((n, MLA_HEADS * MLA_PAD), BF16),
                   jax.ShapeDtypeStruct((n, MLA_HEADS * MLA_PAD), BF16),
                   jax.ShapeDtypeStruct((n, MLA_HEADS * MLA_V), BF16)),
        grid=(n // tm,),
        in_specs=[
            pl.BlockSpec((tm, 512), lambda i: (i, col_blk)),
            full((1, MLA_Q_RANK)),
            full((1, MLA_KV_RANK)),
            full(wq.shape), full(wk.shape), full(wv.shape), full(e.shape),
            pl.BlockSpec((tm, LANES), lambda i: (i % nsb, 0)),
            pl.BlockSpec((tm, LANES), lambda i: (i % nsb, 0)),
        ],
        out_specs=(pl.BlockSpec((tm, MLA_HEADS * MLA_PAD), lambda i: (i, 0)),
                   pl.BlockSpec((tm, MLA_HEADS * MLA_PAD), lambda i: (i, 0)),
                   pl.BlockSpec((tm, MLA_HEADS * MLA_V), lambda i: (i, 0))),
        compiler_params=_params("parallel"),
        name="mla_prep",
    )(hcat, q_norm.reshape(1, -1), kv_norm.reshape(1, -1),
      wq.astype(BF16), wk.astype(BF16), wv.astype(BF16), e, cos_t, sin_t)


def _flash_kernel(q_ref, k_ref, v_ref, o_ref, *, tk, dv):
    tq = q_ref.shape[0]
    nk = k_ref.shape[0] // tk

    def step(ki, carry):
        off = pl.multiple_of(ki * tk, tk)
        new = []
        for hh in range(2):
            m, l, acc = carry[hh]
            q = q_ref[:, hh * MLA_PAD:(hh + 1) * MLA_PAD]
            k = k_ref[pl.ds(off, tk), hh * MLA_PAD:(hh + 1) * MLA_PAD]
            v = v_ref[pl.ds(off, tk), hh * dv:(hh + 1) * dv]
            s = lax.dot_general(q, k, (((1,), (1,)), ((), ())),
                                preferred_element_type=F32)
            m_new = jnp.maximum(m, jnp.max(s, axis=-1, keepdims=True))
            alpha = jnp.exp2(m - m_new)
            p = jnp.exp2(s - m_new)
            l = alpha * l + jnp.sum(p, axis=-1, keepdims=True)
            acc = alpha * acc + jnp.dot(p.astype(BF16), v, preferred_element_type=F32)
            new.append((m_new, l, acc))
        return tuple(new)

    init = tuple((jnp.full((tq, 1), NEG_INF, F32), jnp.zeros((tq, 1), F32),
                  jnp.zeros((tq, dv), F32)) for _ in range(2))
    res = lax.fori_loop(0, nk, step, init)
    o_ref[...] = jnp.concatenate([acc / l for (_, l, acc) in res], axis=-1).astype(o_ref.dtype)


def flash_attention(q, k, v, *, batch, seq, tq=256, tk=1024):
    n = q.shape[0]
    dv = MLA_V
    pairs = MLA_HEADS // 2
    tq = min(tq, seq)
    tk = min(tk, seq)
    nq = seq // tq
    return pl.pallas_call(
        functools.partial(_flash_kernel, tk=tk, dv=dv),
        out_shape=jax.ShapeDtypeStruct((n, MLA_HEADS * dv), BF16),
        grid=(batch, pairs, nq),
        in_specs=[
            pl.BlockSpec((tq, 2 * MLA_PAD), lambda b, p, i: (b * nq + i, p)),
            pl.BlockSpec((seq, 2 * MLA_PAD), lambda b, p, i: (b, p)),
            pl.BlockSpec((seq, 2 * dv), lambda b, p, i: (b, p)),
        ],
        out_specs=pl.BlockSpec((tq, 2 * dv), lambda b, p, i: (b * nq + i, p)),
        compiler_params=_params("parallel", "parallel", "arbitrary"),
        name="flash_attention",
    )(q, k, v)


def _dilated_kernel(*refs, sub_len, first, last):
    if first:
        (q_ref, kp_ref, kc_ref, kn_ref, vp_ref, vc_ref, vn_ref,
         acc_o, m_o, l_o, kwin, vwin) = refs
        acc_i = m_i = l_i = None
    elif last:
        (q_ref, kp_ref, kc_ref, kn_ref, vp_ref, vc_ref, vn_ref, acc_i, m_i, l_i,
         out_o, kwin, vwin) = refs
    else:
        (q_ref, kp_ref, kc_ref, kn_ref, vp_ref, vc_ref, vn_ref, acc_i, m_i, l_i,
         acc_o, m_o, l_o, kwin, vwin) = refs

    t = q_ref.shape[0]
    i = pl.program_id(2)
    kwin[0:DIL_HALF] = kp_ref[...]
    kwin[DIL_HALF:DIL_HALF + t] = kc_ref[...]
    kwin[DIL_HALF + t:] = kn_ref[...]
    vwin[0:DIL_HALF] = vp_ref[...]
    vwin[DIL_HALF:DIL_HALF + t] = vc_ref[...]
    vwin[DIL_HALF + t:] = vn_ref[...]

    def sub_block(sb, carry):
        off = pl.multiple_of(sb * DIL_QBLK, DIL_QBLK)
        rows = pl.ds(off, DIL_QBLK)
        krows = pl.ds(off, DIL_KBLK)
        base = i * t + off - DIL_HALF
        jq = lax.broadcasted_iota(jnp.int32, (DIL_QBLK, DIL_KBLK), 0)
        jk = lax.broadcasted_iota(jnp.int32, (DIL_QBLK, DIL_KBLK), 1)
        rel = jk - jq
        kpos = base + jk
        valid = (rel >= 0) & (rel <= 2 * DIL_HALF) & (kpos >= 0) & (kpos < sub_len)
        bias = jnp.where(valid, 0.0, NEG_INF).astype(F32)
        lane = _lane_iota((DIL_QBLK, LANES))
        m_all = jnp.zeros((DIL_QBLK, LANES), F32)
        l_all = jnp.zeros((DIL_QBLK, LANES), F32)
        pair = []
        for h in range(DIL_HEADS):
            hs = slice(h * HEAD_DIM, (h + 1) * HEAD_DIM)
            q = q_ref[rows, hs]
            k = kwin[krows, hs]
            v = vwin[krows, hs]
            s = lax.dot_general(q, k, (((1,), (1,)), ((), ())),
                                preferred_element_type=F32) + bias
            m_cur = jnp.max(s, axis=-1, keepdims=True)
            if first:
                m_new = m_cur
            else:
                m_old = m_i[rows, h:h + 1]
                m_new = jnp.maximum(m_old, m_cur)
            p = jnp.exp2(s - m_new)
            l_new = jnp.sum(p, axis=-1, keepdims=True)
            acc = jnp.dot(p.astype(BF16), v, preferred_element_type=F32)
            if not first:
                alpha = jnp.exp2(m_old - m_new)
                l_new = l_new + alpha * l_i[rows, h:h + 1]
                acc = acc + alpha * acc_i[rows, hs]
            if last:
                pair.append(acc / l_new)
                if len(pair) == 2:
                    out_o[rows, (h - 1) * HEAD_DIM:(h + 1) * HEAD_DIM] = (
                        jnp.concatenate(pair, axis=-1).astype(out_o.dtype))
                    pair = []
            else:
                acc_o[rows, hs] = acc
                m_all = jnp.where(lane == h, m_new, m_all)
                l_all = jnp.where(lane == h, l_new, l_all)
        if not last:
            m_o[rows, :] = m_all
            l_o[rows, :] = l_all
        return carry

    lax.fori_loop(0, t // DIL_QBLK, sub_block, 0)


def dilated_stage(hcat, state, *, group, batch, seq, t_max=512):
    n = hcat.shape[0]
    width = DIL_HEADS * HEAD_DIM
    d = DIL_PAIRS[group][1]
    assert DIL_PAIRS[group][0] // d == 2 * DIL_HALF
    first = state is None
    last = group == len(DIL_PAIRS) - 1
    sub_len = seq // d
    t = min(t_max, sub_len)
    nt = sub_len // t
    hb = t // DIL_HALF
    nhb = sub_len // DIL_HALF
    ncol = hcat.shape[1] // width
    view = hcat.reshape(n // d, d * hcat.shape[1])

    def cur(kind):
        return pl.BlockSpec((t, width), lambda b, r, i: (b * nt + i, r * ncol + group * 3 + kind))

    def prev(kind):
        return pl.BlockSpec(
            (DIL_HALF, width),
            lambda b, r, i: (b * nhb + jnp.maximum(i * hb - 1, 0), r * ncol + group * 3 + kind))

    def nxt(kind):
        return pl.BlockSpec(
            (DIL_HALF, width),
            lambda b, r, i: (b * nhb + jnp.minimum((i + 1) * hb, nhb - 1), r * ncol + group * 3 + kind))

    acc_spec = pl.BlockSpec((t, width), lambda b, r, i: (b * nt + i, r))
    st_spec = pl.BlockSpec((t, LANES), lambda b, r, i: (b * nt + i, r))
    in_specs = [cur(0), prev(1), cur(1), nxt(1), prev(2), cur(2), nxt(2)]
    args = [view] * 7
    if not first:
        acc, m, l = state
        in_specs += [acc_spec, st_spec, st_spec]
        args += [acc.reshape(n // d, d * width), m.reshape(n // d, d * LANES),
                 l.reshape(n // d, d * LANES)]
    if last:
        out_shape = jax.ShapeDtypeStruct((n // d, d * width), BF16)
        out_specs = acc_spec
    else:
        out_shape = (jax.ShapeDtypeStruct((n // d, d * width), F32),
                     jax.ShapeDtypeStruct((n // d, d * LANES), F32),
                     jax.ShapeDtypeStruct((n // d, d * LANES), F32))
        out_specs = (acc_spec, st_spec, st_spec)
    res = pl.pallas_call(
        functools.partial(_dilated_kernel, sub_len=sub_len, first=first, last=last),
        out_shape=out_shape,
        grid=(batch, d, nt),
        in_specs=in_specs,
        out_specs=out_specs,
        scratch_shapes=[pltpu.VMEM((t + 2 * DIL_HALF, width), BF16),
                        pltpu.VMEM((t + 2 * DIL_HALF, width), BF16)],
        compiler_params=_params("parallel", "parallel", "arbitrary"),
        name=f"dilated_stage{group}",
    )(*args)
    if last:
        return res.reshape(n, width)
    acc, m, l = res
    return acc.reshape(n, width), m.reshape(n, LANES), l.reshape(n, LANES)


def _proj_residual_kernel(*refs, n_in):
    a_refs = refs[:n_in]
    w_refs = refs[n_in:2 * n_in]
    x_ref = refs[2 * n_in]
    o_ref = refs[2 * n_in + 1]
    acc = x_ref[...]
    for a, w in zip(a_refs, w_refs):
        acc = acc + jnp.dot(a[...], w[...], preferred_element_type=F32)
    o_ref[...] = acc


def proj_residual(parts, weights, x, *, tm=512):
    n, d = x.shape
    n_in = len(parts)
    in_specs = ([pl.BlockSpec((tm, a.shape[1]), lambda i: (i, 0)) for a in parts]
                + [pl.BlockSpec(w.shape, lambda i: (0, 0)) for w in weights]
                + [pl.BlockSpec((tm, d), lambda i: (i, 0))])
    return pl.pallas_call(
        functools.partial(_proj_residual_kernel, n_in=n_in),
        out_shape=jax.ShapeDtypeStruct((n, d), F32),
        grid=(n // tm,),
        in_specs=in_specs,
        out_specs=pl.BlockSpec((tm, d), lambda i: (i, 0)),
        compiler_params=_params("parallel"),
        name="proj_residual",
    )(*parts, *weights, x)


def _mlp_kernel(x_ref, g_ref, w1_ref, w2_ref, gf_ref, o_ref, xn_ref, acc_ref, *, final_norm):
    j = pl.program_id(1)

    @pl.when(j == 0)
    def _():
        xn_ref[...] = _rms(x_ref[...], g_ref[...]).astype(BF16)
        acc_ref[...] = jnp.zeros_like(acc_ref)

    h = jnp.dot(xn_ref[...], w1_ref[...], preferred_element_type=F32)
    h = jnp.square(jnp.maximum(h, 0.0)).astype(BF16)
    acc_ref[...] += jnp.dot(h, w2_ref[...], preferred_element_type=F32)

    @pl.when(j == pl.num_programs(1) - 1)
    def _():
        y = x_ref[...] + acc_ref[...]
        if final_norm:
            y = _rms(y, gf_ref[...])
        o_ref[...] = y


def mlp(x, g, w1, w2, g_final, *, final_norm, tm=512, tf=1024):
    n, d = x.shape
    ff = w1.shape[1]
    return pl.pallas_call(
        functools.partial(_mlp_kernel, final_norm=final_norm),
        out_shape=jax.ShapeDtypeStruct((n, d), F32),
        grid=(n // tm, ff // tf),
        in_specs=[
            pl.BlockSpec((tm, d), lambda i, j: (i, 0)),
            pl.BlockSpec((1, d), lambda i, j: (0, 0)),
            pl.BlockSpec((d, tf), lambda i, j: (0, j)),
            pl.BlockSpec((tf, d), lambda i, j: (j, 0)),
            pl.BlockSpec((1, d), lambda i, j: (0, 0)),
        ],
        out_specs=pl.BlockSpec((tm, d), lambda i, j: (i, 0)),
        scratch_shapes=[pltpu.VMEM((tm, d), BF16), pltpu.VMEM((tm, d), F32)],
        compiler_params=_params("parallel", "arbitrary"),
        name="mlp",
    )(x, g.reshape(1, d), w1, w2, g_final.reshape(1, d))


def _rope_lane_tables(seq, dim, lane_lo, lane_hi, period):
    half = dim // 2
    inv = 1.0 / (ROPE_THETA ** (jnp.arange(0, dim, 2, dtype=F32) / dim))
    ang = jnp.arange(seq, dtype=F32)[:, None] * inv[None, :]
    cos, sin = jnp.cos(ang), jnp.sin(ang)
    lane = np.arange(LANES)
    pos = lane % period
    active = (pos >= lane_lo) & (pos < lane_hi)
    idx = (pos - lane_lo) % half
    second = ((pos - lane_lo) % dim) >= half
    cos_t = jnp.where(active[None, :], cos[:, idx], 1.0)
    sin_t = jnp.where(active[None, :], jnp.where(second[None, :], sin[:, idx], -sin[:, idx]), 0.0)
    return cos_t.astype(F32), sin_t.astype(F32)


def kernel(x, norm_mix, norm_mlp, norm_final, ev_w_in, ev_rpb, ev_q_norm, ev_w_uq,
           ev_kv_norm, ev_w_ukv, ev_w_o, od_w_in, od_w_o, mlp_w1, mlp_w2):
    batch, seq, d = x.shape
    n = batch * seq
    depth = norm_mix.shape[0]
    rows = seq // GRID_W

    cos_f, sin_f = _rope_lane_tables(seq, HEAD_DIM, 0, HEAD_DIM, HEAD_DIM)
    cos_r, sin_r = _rope_lane_tables(seq, MLA_ROPE, MLA_NOPE, MLA_NOPE + MLA_ROPE, MLA_PAD)
    mla_q_scale = (MLA_NOPE + MLA_ROPE) ** -0.5 * LOG2E
    dil_q_scale = HEAD_DIM ** -0.5 * LOG2E

    xs = x.reshape(n, d)
    for layer in range(depth):
        idx = layer // 2
        if layer % 2 == 0:
            w_in = jnp.pad(ev_w_in[idx], ((0, 0), (0, EVEN_IN_PAD - ev_w_in.shape[2]))).astype(BF16)
            hcat = norm_matmul(xs, norm_mix[layer], w_in)
            bias = _na_bias_table(ev_rpb[idx], rows)
            out_a = na_attention(hcat, bias, batch=batch, seq=seq)
            q, k, v = mla_prep(hcat, ev_q_norm[idx], ev_w_uq[idx], ev_kv_norm[idx],
                               ev_w_ukv[idx], cos_r, sin_r, seq=seq, q_scale=mla_q_scale)
            out_b = flash_attention(q, k, v, batch=batch, seq=seq)
            w_o = ev_w_o[idx].astype(BF16)
            xs = proj_residual([out_a, out_b], [w_o[:NA_WIDTH], w_o[NA_WIDTH:]], xs)
        else:
            hcat = norm_matmul(xs, norm_mix[layer], od_w_in[idx].astype(BF16),
                               rope=(cos_f, sin_f), q_scale=dil_q_scale, seq=seq)
            state = None
            for g in range(len(DIL_PAIRS)):
                state = dilated_stage(hcat, state, group=g, batch=batch, seq=seq)
            xs = proj_residual([state], [od_w_o[idx].astype(BF16)], xs)
        xs = mlp(xs, norm_mlp[layer], mlp_w1[layer].astype(BF16), mlp_w2[layer].astype(BF16),
                 norm_final, final_norm=(layer == depth - 1))
    return xs.reshape(batch, seq, d)
```

```python
import functools
import math

import numpy as np
import jax
import jax.numpy as jnp
from jax import lax
from jax.experimental import pallas as pl
from jax.experimental.pallas import tpu as pltpu

F32 = jnp.float32
BF16 = jnp.bfloat16

D_MODEL = 1024
HEAD_DIM = 64
ROPE_THETA = 10000.0
RMS_EPS = 1e-6
NEG_INF = -1e30
LOG2E = math.log2(math.e)

GRID_W = 64
NA_HEADS = 8
NA_ROWS = 8
NA_COLS = 16
NA_WIDTH = NA_HEADS * HEAD_DIM

MLA_HEADS = 8
MLA_Q_RANK = 256
MLA_KV_RANK = 128
MLA_NOPE = 64
MLA_ROPE = 32
MLA_V = 64
MLA_PAD = 128
MLA_IN = 512

DIL_PAIRS = ((128, 1), (512, 4), (2048, 16))
DIL_HEADS = D_MODEL // HEAD_DIM
DIL_HALF = 64
DIL_QBLK = 128
DIL_KBLK = 256
DIL_HGRP = 4
DIL_TOK = DIL_QBLK * max(d for _, d in DIL_PAIRS)
DIL_ORDER = (2, 1, 0)

D_FF = 4 * D_MODEL

LANES = 128
VMEM_LIMIT_BYTES = 56 * 1024 * 1024


def _params(*semantics):
    return pltpu.CompilerParams(dimension_semantics=semantics,
                                vmem_limit_bytes=VMEM_LIMIT_BYTES)


def _rms(x, g):
    ms = jnp.mean(x * x, axis=-1, keepdims=True)
    return x * lax.rsqrt(ms + RMS_EPS) * g


def _lane_iota(shape):
    return lax.broadcasted_iota(jnp.int32, shape, len(shape) - 1)


def _rope_chunk(x, cos, sin_signed, half):
    lane = _lane_iota(x.shape)
    fwd = pltpu.roll(x, LANES - half, 1)
    bwd = pltpu.roll(x, half, 1)
    rot = jnp.where((lane % (2 * half)) < half, fwd, bwd)
    return x * cos + rot * sin_signed


def _full_spec(shape):
    return pl.BlockSpec(shape, lambda *_: (0,) * len(shape))


def _even_in_proj_kernel(x_ref, g_ref, w_ref, na_ref, h_ref):
    xn = _rms(x_ref[...], g_ref[...]).astype(BF16)
    acc = jnp.dot(xn, w_ref[...], preferred_element_type=F32)
    n_na = na_ref.shape[0]
    q_slabs = n_na // 3
    for c in range(n_na):
        y = acc[:, c * LANES:(c + 1) * LANES]
        if c < q_slabs:
            y = y * (HEAD_DIM ** -0.5)
        na_ref[c] = y.astype(na_ref.dtype)
    h_ref[...] = acc[:, n_na * LANES:].astype(h_ref.dtype)


def even_in_proj(x, g, w, *, tm=512):
    n, d = x.shape
    f = w.shape[1]
    n_na = 3 * NA_WIDTH // LANES
    assert f == n_na * LANES + MLA_IN
    return pl.pallas_call(
        _even_in_proj_kernel,
        out_shape=(jax.ShapeDtypeStruct((n_na, n, LANES), BF16),
                   jax.ShapeDtypeStruct((n, MLA_IN), BF16)),
        grid=(n // tm,),
        in_specs=[pl.BlockSpec((tm, d), lambda i: (i, 0)),
                  _full_spec((1, d)), _full_spec((d, f))],
        out_specs=(pl.BlockSpec((n_na, tm, LANES), lambda i: (0, i, 0)),
                   pl.BlockSpec((tm, MLA_IN), lambda i: (i, 0))),
        compiler_params=_params("parallel"),
        name="even_in_proj",
    )(x, g.reshape(1, d), w)


def _na_kernel(q_ref, k_ref, v_ref, b_ref, o_ref, *, rows):
    wr = NA_ROWS
    kw = wr * GRID_W

    def row(r, carry):
        r_start = jnp.clip(r - wr // 2, 0, rows - wr)
        didx = r - r_start
        qoff = pl.multiple_of(r * GRID_W, GRID_W)
        koff = pl.multiple_of(r_start * GRID_W, GRID_W)
        outs = []
        for hh in range(2):
            hs = slice(hh * HEAD_DIM, (hh + 1) * HEAD_DIM)
            q = q_ref[pl.ds(qoff, GRID_W), hs]
            k = k_ref[pl.ds(koff, kw), hs]
            v = v_ref[pl.ds(koff, kw), hs]
            s = lax.dot_general(q, k, (((1,), (1,)), ((), ())),
                                preferred_element_type=F32)
            s = s + b_ref[hh, didx]
            m = jnp.max(s, axis=-1, keepdims=True)
            e = jnp.exp(s - m)
            z = jnp.sum(e, axis=-1, keepdims=True)
            o = jnp.dot(e.astype(BF16), v, preferred_element_type=F32)
            outs.append(o / z)
        o_ref[pl.ds(qoff, GRID_W), :] = jnp.concatenate(outs, axis=-1).astype(o_ref.dtype)
        return carry

    lax.fori_loop(0, rows, row, 0)


def _na_bias_table(rpb, rows):
    assert rows >= NA_ROWS
    cols = np.arange(GRID_W)
    col_start = np.clip(cols - NA_COLS // 2, 0, GRID_W - NA_COLS)
    kc = np.arange(GRID_W)
    inside = (kc[None, :] >= col_start[:, None]) & (kc[None, :] < col_start[:, None] + NA_COLS)
    col_rel = kc[None, :] - cols[:, None] + (NA_COLS - 1)
    sel = (col_rel[:, :, None] == np.arange(2 * NA_COLS - 1)[None, None, :]) & inside[:, :, None]
    delta = np.arange(NA_ROWS)
    i = np.arange(NA_ROWS)
    row_rel = i[None, :] - delta[:, None] + (NA_ROWS - 1)
    by_row = rpb.astype(F32)[:, row_rel]
    b = jnp.einsum('hdiv,ckv->hdcik', by_row, jnp.asarray(sel, F32),
                   precision=lax.Precision.HIGHEST)
    b = b + jnp.asarray(np.where(inside, 0.0, NEG_INF), F32)[None, None, :, None, :]
    return b.reshape(rpb.shape[0], NA_ROWS, GRID_W, NA_ROWS * GRID_W)


def na_attention(qkv, bias, *, batch, seq):
    n = qkv.shape[1]
    rows = seq // GRID_W
    pairs = NA_HEADS // 2
    blk = (None, seq, LANES)
    return pl.pallas_call(
        functools.partial(_na_kernel, rows=rows),
        out_shape=jax.ShapeDtypeStruct((pairs, n, LANES), BF16),
        grid=(batch, pairs),
        in_specs=[
            pl.BlockSpec(blk, lambda b, p: (p, b, 0)),
            pl.BlockSpec(blk, lambda b, p: (pairs + p, b, 0)),
            pl.BlockSpec(blk, lambda b, p: (2 * pairs + p, b, 0)),
            pl.BlockSpec((2, NA_ROWS, GRID_W, NA_ROWS * GRID_W), lambda b, p: (p, 0, 0, 0)),
        ],
        out_specs=pl.BlockSpec(blk, lambda b, p: (p, b, 0)),
        compiler_params=_params("parallel", "parallel"),
        name="na_attention",
    )(qkv, qkv, qkv, bias)


def _mla_prep_kernel(h_ref, qn_ref, kvn_ref, wq_ref, wk_ref, wv_ref, e_ref,
                     cos_ref, sin_ref, q_ref, k_ref, v_ref, *, q_scale):
    h = h_ref[...].astype(F32)
    o1 = MLA_Q_RANK
    o2 = o1 + MLA_KV_RANK
    c_q = _rms(h[:, :o1], qn_ref[...]).astype(BF16)
    c_kv = _rms(h[:, o1:o2], kvn_ref[...]).astype(BF16)
    k_pe = h_ref[:, o2:o2 + MLA_ROPE]
    q = jnp.dot(c_q, wq_ref[...], preferred_element_type=F32)
    k = (jnp.dot(c_kv, wk_ref[...], preferred_element_type=F32)
         + jnp.dot(k_pe, e_ref[...], preferred_element_type=F32))
    v = jnp.dot(c_kv, wv_ref[...], preferred_element_type=F32)
    cos = cos_ref[...]
    sin = sin_ref[...]
    for p in range(MLA_HEADS // 2):
        qs, ks = [], []
        for hd in (2 * p, 2 * p + 1):
            sl = slice(hd * MLA_PAD, (hd + 1) * MLA_PAD)
            qs.append(_rope_chunk(q[:, sl], cos, sin, MLA_ROPE // 2) * q_scale)
            ks.append(_rope_chunk(k[:, sl], cos, sin, MLA_ROPE // 2))
        q_ref[p] = jnp.concatenate(qs, axis=-1).astype(q_ref.dtype)
        k_ref[p] = jnp.concatenate(ks, axis=-1).astype(k_ref.dtype)
        v_ref[p] = v[:, 2 * p * MLA_V:(2 * p + 2) * MLA_V].astype(v_ref.dtype)


def mla_prep(h, q_norm, w_uq, kv_norm, w_ukv, cos_t, sin_t, *, seq, q_scale, tm=512):
    n = h.shape[0]
    hq = MLA_NOPE + MLA_ROPE
    pairs = MLA_HEADS // 2
    wq = w_uq.reshape(MLA_Q_RANK, MLA_HEADS, hq)
    wq = jnp.pad(wq, ((0, 0), (0, 0), (0, MLA_PAD - hq))).reshape(MLA_Q_RANK, MLA_HEADS * MLA_PAD)
    wkv = w_ukv.reshape(MLA_KV_RANK, MLA_HEADS, MLA_NOPE + MLA_V)
    wk = jnp.pad(wkv[:, :, :MLA_NOPE], ((0, 0), (0, 0), (0, MLA_PAD - MLA_NOPE)))
    wk = wk.reshape(MLA_KV_RANK, MLA_HEADS * MLA_PAD)
    wv = wkv[:, :, MLA_NOPE:].reshape(MLA_KV_RANK, MLA_HEADS * MLA_V)
    e = np.zeros((MLA_ROPE, MLA_HEADS, MLA_PAD), np.float32)
    for t in range(MLA_ROPE):
        e[t, :, MLA_NOPE + t] = 1.0
    e = jnp.asarray(e.reshape(MLA_ROPE, MLA_HEADS * MLA_PAD), BF16)
    nsb = seq // tm
    return pl.pallas_call(
        functools.partial(_mla_prep_kernel, q_scale=q_scale),
        out_shape=(jax.ShapeDtypeStruct((pairs, n, 2 * MLA_PAD), BF16),
                   jax.ShapeDtypeStruct((pairs, n, 2 * MLA_PAD), BF16),
                   jax.ShapeDtypeStruct((pairs, n, 2 * MLA_V), BF16)),
        grid=(n // tm,),
        in_specs=[
            pl.BlockSpec((tm, MLA_IN), lambda i: (i, 0)),
            _full_spec((1, MLA_Q_RANK)),
            _full_spec((1, MLA_KV_RANK)),
            _full_spec(wq.shape), _full_spec(wk.shape), _full_spec(wv.shape), _full_spec(e.shape),
            pl.BlockSpec((tm, LANES), lambda i: (i % nsb, 0)),
            pl.BlockSpec((tm, LANES), lambda i: (i % nsb, 0)),
        ],
        out_specs=(pl.BlockSpec((pairs, tm, 2 * MLA_PAD), lambda i: (0, i, 0)),
                   pl.BlockSpec((pairs, tm, 2 * MLA_PAD), lambda i: (0, i, 0)),
                   pl.BlockSpec((pairs, tm, 2 * MLA_V), lambda i: (0, i, 0))),
        compiler_params=_params("parallel"),
        name="mla_prep",
    )(h, q_norm.reshape(1, -1), kv_norm.reshape(1, -1),
      wq.astype(BF16), wk.astype(BF16), wv.astype(BF16), e, cos_t, sin_t)


def _flash_kernel(q_ref, k_ref, v_ref, o_ref, *, tk, dv):
    tq = q_ref.shape[0]
    nk = k_ref.shape[0] // tk

    def step(ki, carry):
        off = pl.multiple_of(ki * tk, tk)
        new = []
        for hh in range(2):
            m, l, acc = carry[hh]
            q = q_ref[:, hh * MLA_PAD:(hh + 1) * MLA_PAD]
            k = k_ref[pl.ds(off, tk), hh * MLA_PAD:(hh + 1) * MLA_PAD]
            v = v_ref[pl.ds(off, tk), hh * dv:(hh + 1) * dv]
            s = lax.dot_general(q, k, (((1,), (1,)), ((), ())),
                                preferred_element_type=F32)
            m_new = jnp.maximum(m, jnp.max(s, axis=-1, keepdims=True))
            alpha = jnp.exp2(m - m_new)
            p = jnp.exp2(s - m_new)
            l = alpha * l + jnp.sum(p, axis=-1, keepdims=True)
            acc = alpha * acc + jnp.dot(p.astype(BF16), v, preferred_element_type=F32)
            new.append((m_new, l, acc))
        return tuple(new)

    init = tuple((jnp.full((tq, 1), NEG_INF, F32), jnp.zeros((tq, 1), F32),
                  jnp.zeros((tq, dv), F32)) for _ in range(2))
    res = lax.fori_loop(0, nk, step, init)
    o_ref[...] = jnp.concatenate([acc / l for (_, l, acc) in res], axis=-1).astype(o_ref.dtype)


def flash_attention(q, k, v, *, batch, seq, tq=256, tk=1024):
    pairs, n, _ = q.shape
    dv = MLA_V
    tq = min(tq, seq)
    tk = min(tk, seq)
    nq = seq // tq
    return pl.pallas_call(
        functools.partial(_flash_kernel, tk=tk, dv=dv),
        out_shape=jax.ShapeDtypeStruct((pairs, n, 2 * dv), BF16),
        grid=(batch, pairs, nq),
        in_specs=[
            pl.BlockSpec((None, tq, 2 * MLA_PAD), lambda b, p, i: (p, b * nq + i, 0)),
            pl.BlockSpec((None, seq, 2 * MLA_PAD), lambda b, p, i: (p, b, 0)),
            pl.BlockSpec((None, seq, 2 * dv), lambda b, p, i: (p, b, 0)),
        ],
        out_specs=pl.BlockSpec((None, tq, 2 * dv), lambda b, p, i: (p, b * nq + i, 0)),
        compiler_params=_params("parallel", "parallel", "arbitrary"),
        name="flash_attention",
    )(q, k, v)


def _row_permutation(tm, d):
    p = np.zeros((tm, tm), np.float32)
    rho = np.arange(tm)
    r, a = rho // (tm // d), rho % (tm // d)
    p[rho, a * d + r] = 1.0
    return jnp.asarray(p, BF16)


def _dilated_in_proj_kernel(*refs, d, q_scale):
    if d > 1:
        x_ref, g_ref, w_ref, cos_ref, sin_ref, perm_ref, o_ref, xn_ref = refs
    else:
        x_ref, g_ref, w_ref, cos_ref, sin_ref, o_ref, xn_ref = refs
    kind = pl.program_id(1)
    tm = x_ref.shape[0]

    @pl.when(kind == 0)
    def _():
        xn_ref[...] = _rms(x_ref[...], g_ref[...]).astype(BF16)

    acc = jnp.dot(xn_ref[...], w_ref[...], preferred_element_type=F32)

    def emit(y):
        if d > 1:
            y = jnp.dot(perm_ref[...], y, preferred_element_type=F32).astype(o_ref.dtype)
        rows = tm // d
        width = y.shape[1] // DIL_HGRP
        lead = o_ref.shape[1]
        per = rows // lead
        for hg in range(DIL_HGRP):
            for t in range(lead):
                for r in range(d):
                    lo = r * rows + t * per
                    o_ref[hg, t, r] = y[lo:lo + per, hg * width:(hg + 1) * width]

    def roped(scale):
        cos = cos_ref[...]
        sin = sin_ref[...]
        parts = []
        for c in range(acc.shape[1] // LANES):
            y = _rope_chunk(acc[:, c * LANES:(c + 1) * LANES], cos, sin, HEAD_DIM // 2)
            if scale != 1.0:
                y = y * scale
            parts.append(y.astype(o_ref.dtype))
        emit(jnp.concatenate(parts, axis=-1))

    @pl.when(kind == 0)
    def _():
        roped(q_scale)

    @pl.when(kind == 1)
    def _():
        roped(1.0)

    @pl.when(kind == 2)
    def _():
        emit(acc.astype(o_ref.dtype))


def dilated_in_proj(x, g, w, cos_t, sin_t, *, group, seq, q_scale, tm=512):
    n, dm = x.shape
    d = DIL_PAIRS[group][1]
    width = DIL_HEADS * HEAD_DIM
    pt = DIL_QBLK * d
    hw = width // DIL_HGRP
    nsb = seq // tm
    in_specs = [
        pl.BlockSpec((tm, dm), lambda i, k: (i, 0)),
        _full_spec((1, dm)),
        pl.BlockSpec((None, dm, width), lambda i, k: (3 * group + k, 0, 0)),
        pl.BlockSpec((tm, LANES), lambda i, k: (i % nsb, 0)),
        pl.BlockSpec((tm, LANES), lambda i, k: (i % nsb, 0)),
    ]
    args = [x, g.reshape(1, dm), w, cos_t, sin_t]
    if d > 1:
        in_specs.append(_full_spec((tm, tm)))
        args.append(_row_permutation(tm, d))
    if pt >= tm:
        sub = pt // tm
        out_spec = pl.BlockSpec((None, DIL_HGRP, 1, d, tm // d, hw),
                                lambda i, k: (k, 0, i // sub, 0, i % sub, 0))
    else:
        lead = tm // pt
        out_spec = pl.BlockSpec((None, DIL_HGRP, lead, d, DIL_QBLK, hw),
                                lambda i, k: (k, 0, i, 0, 0, 0))
    return pl.pallas_call(
        functools.partial(_dilated_in_proj_kernel, d=d, q_scale=q_scale),
        out_shape=jax.ShapeDtypeStruct((3, DIL_HGRP, n // pt, d, DIL_QBLK, hw), BF16),
        grid=(n // tm, 3),
        in_specs=in_specs,
        out_specs=out_spec,
        scratch_shapes=[pltpu.VMEM((tm, dm), BF16)],
        compiler_params=_params("parallel", "arbitrary"),
        name=f"dilated_in_proj{group}",
    )(*args)


def _dilated_kernel(*refs, seq):
    ng = len(DIL_PAIRS)
    ins = [refs[7 * g:7 * g + 7] for g in range(ng)]
    out_ref, acc_scr, m_scr, l_scr = refs[7 * ng:]
    i = pl.program_id(1)
    n_pairs = DIL_HGRP // 2
    lane = _lane_iota((DIL_QBLK, LANES))
    jq = lax.broadcasted_iota(jnp.int32, (DIL_QBLK, DIL_KBLK), 0)
    jk = lax.broadcasted_iota(jnp.int32, (DIL_QBLK, DIL_KBLK), 1)
    rel = jk - jq
    in_band = (rel >= 0) & (rel <= 2 * DIL_HALF)

    for step, g in enumerate(DIL_ORDER):
        d = DIL_PAIRS[g][1]
        q_ref, kp_ref, kc_ref, kn_ref, vp_ref, vc_ref, vn_ref = ins[g]
        nt = kc_ref.shape[0]
        sub_len = seq // d
        first = step == 0
        last = step == ng - 1

        def window(p_ref, c_ref, n_ref, t, r):
            if nt == 1:
                before, after = p_ref[r], n_ref[r]
            else:
                before = jnp.where(t == 0, p_ref[r],
                                   c_ref[jnp.maximum(t - 1, 0), r, DIL_HALF:, :])
                after = jnp.where(t == nt - 1, n_ref[r],
                                  c_ref[jnp.minimum(t + 1, nt - 1), r, :DIL_HALF, :])
            return jnp.concatenate([before, c_ref[t, r], after], axis=0)

        def unit(idx, carry, d=d, nt=nt, sub_len=sub_len, first=first, last=last,
                 q_ref=q_ref, kp_ref=kp_ref, kc_ref=kc_ref, kn_ref=kn_ref,
                 vp_ref=vp_ref, vc_ref=vc_ref, vn_ref=vn_ref, window=window):
            t = idx // d
            r = idx % d
            q = q_ref[t, r]
            kwin = window(kp_ref, kc_ref, kn_ref, t, r)
            vwin = window(vp_ref, vc_ref, vn_ref, t, r)
            base = i * (DIL_TOK // d) + t * DIL_QBLK - DIL_HALF
            kpos = base + jk
            valid = in_band & (kpos >= 0) & (kpos < sub_len)
            bias = jnp.where(valid, 0.0, NEG_INF).astype(F32)
            start = t * (DIL_QBLK * d) + r
            rows = pl.ds(start, DIL_QBLK, stride=d) if d > 1 else pl.ds(start, DIL_QBLK)
            if not first:
                m_all = m_scr[rows, :]
                l_all = l_scr[rows, :]
            else:
                m_all = jnp.zeros((DIL_QBLK, LANES), F32)
                l_all = jnp.zeros((DIL_QBLK, LANES), F32)
            for hp in range(n_pairs):
                if not first:
                    acc_old = acc_scr[hp, rows, :]
                pair = []
                for hh in range(2):
                    h = 2 * hp + hh
                    hs = slice(h * HEAD_DIM, (h + 1) * HEAD_DIM)
                    s = lax.dot_general(q[:, hs], kwin[:, hs], (((1,), (1,)), ((), ())),
                                        preferred_element_type=F32) + bias
                    m_new = jnp.max(s, axis=-1, keepdims=True)
                    if not first:
                        m_old = m_all[:, h:h + 1]
                        m_new = jnp.maximum(m_old, m_new)
                    p = jnp.exp2(s - m_new)
                    l_new = jnp.sum(p, axis=-1, keepdims=True)
                    acc = jnp.dot(p.astype(BF16), vwin[:, hs], preferred_element_type=F32)
                    if not first:
                        alpha = jnp.exp2(m_old - m_new)
                        l_new = l_new + alpha * l_all[:, h:h + 1]
                        acc = acc + alpha * acc_old[:, hh * HEAD_DIM:(hh + 1) * HEAD_DIM]
                    if last:
                        acc = acc / l_new
                    else:
                        m_all = jnp.where(lane == h, m_new, m_all)
                        l_all = jnp.where(lane == h, l_new, l_all)
                    pair.append(acc)
                both = jnp.concatenate(pair, axis=-1)
                if last:
                    out_ref[rows, hp * LANES:(hp + 1) * LANES] = both.astype(out_ref.dtype)
                else:
                    acc_scr[hp, rows, :] = both
            if not last:
                m_scr[rows, :] = m_all
                l_scr[rows, :] = l_all
            return carry

        lax.fori_loop(0, nt * d, unit, 0)


def dilated_attention(hq, *, batch, seq):
    assert seq % DIL_TOK == 0
    n = batch * seq
    hw = DIL_HEADS * HEAD_DIM // DIL_HGRP
    nblk = seq // DIL_TOK
    in_specs, args = [], []
    for g, (_, d) in enumerate(DIL_PAIRS):
        pt = DIL_QBLK * d
        nt = DIL_TOK // pt
        tiles = seq // pt

        def cur(kind, nt=nt, d=d):
            return pl.BlockSpec((None, None, nt, d, DIL_QBLK, hw),
                                lambda b, i, hg: (kind, hg, b * nblk + i, 0, 0, 0))

        def prev(kind, nt=nt, d=d, tiles=tiles):
            return pl.BlockSpec(
                (None, None, None, d, DIL_HALF, hw),
                lambda b, i, hg: (kind, hg, b * tiles + jnp.maximum(i * nt - 1, 0), 0, 1, 0))

        def nxt(kind, nt=nt, d=d, tiles=tiles):
            return pl.BlockSpec(
                (None, None, None, d, DIL_HALF, hw),
                lambda b, i, hg: (kind, hg, b * tiles + jnp.minimum((i + 1) * nt, tiles - 1), 0, 0, 0))

        in_specs += [cur(0), prev(1), cur(1), nxt(1), prev(2), cur(2), nxt(2)]
        args += [hq[g]] * 7
    return pl.pallas_call(
        functools.partial(_dilated_kernel, seq=seq),
        out_shape=jax.ShapeDtypeStruct((DIL_HGRP, n, hw), BF16),
        grid=(batch, nblk, DIL_HGRP),
        in_specs=in_specs,
        out_specs=pl.BlockSpec((None, DIL_TOK, hw), lambda b, i, hg: (hg, b * nblk + i, 0)),
        scratch_shapes=[pltpu.VMEM((DIL_HGRP // 2, DIL_TOK, LANES), F32),
                        pltpu.VMEM((DIL_TOK, LANES), F32),
                        pltpu.VMEM((DIL_TOK, LANES), F32)],
        compiler_params=_params("parallel", "parallel", "arbitrary"),
        name="dilated_attention",
    )(*args)


def _proj_residual_kernel(*refs, n_in):
    a_refs = refs[:n_in]
    w_ref, x_ref, o_ref = refs[n_in:]
    pieces = [a[s] for a in a_refs for s in range(a.shape[0])]
    a = jnp.concatenate(pieces, axis=-1)
    o_ref[...] = x_ref[...] + jnp.dot(a, w_ref[...], preferred_element_type=F32)


def proj_residual(parts, w, x, *, tm=512):
    n, d = x.shape
    in_specs = ([pl.BlockSpec((a.shape[0], tm, a.shape[2]), lambda i: (0, i, 0)) for a in parts]
                + [_full_spec(w.shape), pl.BlockSpec((tm, d), lambda i: (i, 0))])
    return pl.pallas_call(
        functools.partial(_proj_residual_kernel, n_in=len(parts)),
        out_shape=jax.ShapeDtypeStruct((n, d), F32),
        grid=(n // tm,),
        in_specs=in_specs,
        out_specs=pl.BlockSpec((tm, d), lambda i: (i, 0)),
        compiler_params=_params("parallel"),
        name="proj_residual",
    )(*parts, w, x)


def _mlp_kernel(x_ref, g_ref, w1_ref, w2_ref, gf_ref, o_ref, xn_ref, acc_ref, *, final_norm):
    j = pl.program_id(1)

    @pl.when(j == 0)
    def _():
        xn_ref[...] = _rms(x_ref[...], g_ref[...]).astype(BF16)
        acc_ref[...] = jnp.zeros_like(acc_ref)

    h = jnp.dot(xn_ref[...], w1_ref[...], preferred_element_type=F32)
    h = jnp.square(jnp.maximum(h, 0.0)).astype(BF16)
    acc_ref[...] += jnp.dot(h, w2_ref[...], preferred_element_type=F32)

    @pl.when(j == pl.num_programs(1) - 1)
    def _():
        y = x_ref[...] + acc_ref[...]
        if final_norm:
            y = _rms(y, gf_ref[...])
        o_ref[...] = y


def mlp(x, g, w1, w2, g_final, *, final_norm, tm=512):
    n, d = x.shape
    nf, _, tf = w1.shape
    return pl.pallas_call(
        functools.partial(_mlp_kernel, final_norm=final_norm),
        out_shape=jax.ShapeDtypeStruct((n, d), F32),
        grid=(n // tm, nf),
        in_specs=[
            pl.BlockSpec((tm, d), lambda i, j: (i, 0)),
            _full_spec((1, d)),
            pl.BlockSpec((None, d, tf), lambda i, j: (j, 0, 0)),
            pl.BlockSpec((tf, d), lambda i, j: (j, 0)),
            _full_spec((1, d)),
        ],
        out_specs=pl.BlockSpec((tm, d), lambda i, j: (i, 0)),
        scratch_shapes=[pltpu.VMEM((tm, d), BF16), pltpu.VMEM((tm, d), F32)],
        compiler_params=_params("parallel", "arbitrary"),
        name="mlp",
    )(x, g.reshape(1, d), w1, w2, g_final.reshape(1, d))


def _column_slabs(w, tf):
    d, f = w.shape
    return w.astype(BF16).reshape(d, f // tf, tf).transpose(1, 0, 2)


def _rope_lane_tables(seq, dim, lane_lo, lane_hi, period):
    half = dim // 2
    inv = 1.0 / (ROPE_THETA ** (jnp.arange(0, dim, 2, dtype=F32) / dim))
    ang = jnp.arange(seq, dtype=F32)[:, None] * inv[None, :]
    cos, sin = jnp.cos(ang), jnp.sin(ang)
    lane = np.arange(LANES)
    pos = lane % period
    active = (pos >= lane_lo) & (pos < lane_hi)
    idx = (pos - lane_lo) % half
    second = ((pos - lane_lo) % dim) >= half
    cos_t = jnp.where(active[None, :], cos[:, idx], 1.0)
    sin_t = jnp.where(active[None, :], jnp.where(second[None, :], sin[:, idx], -sin[:, idx]), 0.0)
    return cos_t.astype(F32), sin_t.astype(F32)


def kernel(x, norm_mix, norm_mlp, norm_final, ev_w_in, ev_rpb, ev_q_norm, ev_w_uq,
           ev_kv_norm, ev_w_ukv, ev_w_o, od_w_in, od_w_o, mlp_w1, mlp_w2):
    batch, seq, d = x.shape
    n = batch * seq
    depth = norm_mix.shape[0]
    rows = seq // GRID_W
    width = DIL_HEADS * HEAD_DIM

    cos_f, sin_f = _rope_lane_tables(seq, HEAD_DIM, 0, HEAD_DIM, HEAD_DIM)
    cos_r, sin_r = _rope_lane_tables(seq, MLA_ROPE, MLA_NOPE, MLA_NOPE + MLA_ROPE, MLA_PAD)
    mla_q_scale = (MLA_NOPE + MLA_ROPE) ** -0.5 * LOG2E
    dil_q_scale = HEAD_DIM ** -0.5 * LOG2E
    even_width = 3 * NA_WIDTH + MLA_IN

    xs = x.reshape(n, d)
    for layer in range(depth):
        idx = layer // 2
        if layer % 2 == 0:
            w_in = jnp.pad(ev_w_in[idx], ((0, 0), (0, even_width - ev_w_in.shape[2]))).astype(BF16)
            na_qkv, latent = even_in_proj(xs, norm_mix[layer], w_in)
            bias = _na_bias_table(ev_rpb[idx], rows)
            out_a = na_attention(na_qkv, bias, batch=batch, seq=seq)
            q, k, v = mla_prep(latent, ev_q_norm[idx], ev_w_uq[idx], ev_kv_norm[idx],
                               ev_w_ukv[idx], cos_r, sin_r, seq=seq, q_scale=mla_q_scale)
            out_b = flash_attention(q, k, v, batch=batch, seq=seq)
            xs = proj_residual([out_a, out_b], ev_w_o[idx].astype(BF16), xs)
        else:
            w_in = _column_slabs(od_w_in[idx], width)
            hq = [dilated_in_proj(xs, norm_mix[layer], w_in, cos_f, sin_f, group=g, seq=seq,
                                  q_scale=dil_q_scale) for g in range(len(DIL_PAIRS))]
            out = dilated_attention(hq, batch=batch, seq=seq)
            xs = proj_residual([out], od_w_o[idx].astype(BF16), xs)
        xs = mlp(xs, norm_mlp[layer], _column_slabs(mlp_w1[layer], 1024),
                 mlp_w2[layer].astype(BF16), norm_final, final_norm=(layer == depth - 1))
    return xs.reshape(batch, seq, d)
```

```python
import functools
import math

import numpy as np
import jax
import jax.numpy as jnp
from jax import lax
from jax.experimental import pallas as pl
from jax.experimental.pallas import tpu as pltpu

F32 = jnp.float32
BF16 = jnp.bfloat16

D_MODEL = 1024
HEAD_DIM = 64
ROPE_THETA = 10000.0
RMS_EPS = 1e-6
NEG_INF = -1e30
LOG2E = math.log2(math.e)

GRID_W = 64
NA_HEADS = 8
NA_ROWS = 8
NA_COLS = 16
NA_WIDTH = NA_HEADS * HEAD_DIM
NA_UNROLL = 8

MLA_HEADS = 8
MLA_Q_RANK = 256
MLA_KV_RANK = 128
MLA_NOPE = 64
MLA_ROPE = 32
MLA_V = 64
MLA_PAD = 128
MLA_IN = 512

DIL_PAIRS = ((128, 1), (512, 4), (2048, 16))
DIL_HEADS = D_MODEL // HEAD_DIM
DIL_HALF = 64
DIL_QBLK = 128
DIL_KBLK = 256
DIL_HGRP = 4
DIL_TOK = DIL_QBLK * max(d for _, d in DIL_PAIRS)
DIL_ORDER = (2, 1, 0)
DIL_UNROLL = 4

D_FF = 4 * D_MODEL

LANES = 128
VMEM_LIMIT_BYTES = 56 * 1024 * 1024


def _params(*semantics):
    return pltpu.CompilerParams(dimension_semantics=semantics,
                                vmem_limit_bytes=VMEM_LIMIT_BYTES)


def _rms(x, g):
    ms = jnp.mean(x * x, axis=-1, keepdims=True)
    return x * lax.rsqrt(ms + RMS_EPS) * g


def _lane_iota(shape):
    return lax.broadcasted_iota(jnp.int32, shape, len(shape) - 1)


def _rope_chunk(x, cos, sin_signed, half):
    lane = _lane_iota(x.shape)
    fwd = pltpu.roll(x, LANES - half, 1)
    bwd = pltpu.roll(x, half, 1)
    rot = jnp.where((lane % (2 * half)) < half, fwd, bwd)
    return x * cos + rot * sin_signed


def _full_spec(shape):
    return pl.BlockSpec(shape, lambda *_: (0,) * len(shape))


def _even_in_proj_kernel(x_ref, g_ref, w_ref, na_ref, h_ref):
    xn = _rms(x_ref[...], g_ref[...]).astype(BF16)
    acc = jnp.dot(xn, w_ref[...], preferred_element_type=F32)
    n_na = na_ref.shape[0]
    q_slabs = n_na // 3
    for c in range(n_na):
        y = acc[:, c * LANES:(c + 1) * LANES]
        if c < q_slabs:
            y = y * (HEAD_DIM ** -0.5 * LOG2E)
        na_ref[c] = y.astype(na_ref.dtype)
    h_ref[...] = acc[:, n_na * LANES:].astype(h_ref.dtype)


def even_in_proj(x, g, w, *, tm=512):
    n, d = x.shape
    f = w.shape[1]
    n_na = 3 * NA_WIDTH // LANES
    assert f == n_na * LANES + MLA_IN
    return pl.pallas_call(
        _even_in_proj_kernel,
        out_shape=(jax.ShapeDtypeStruct((n_na, n, LANES), BF16),
                   jax.ShapeDtypeStruct((n, MLA_IN), BF16)),
        grid=(n // tm,),
        in_specs=[pl.BlockSpec((tm, d), lambda i: (i, 0)),
                  _full_spec((1, d)), _full_spec((d, f))],
        out_specs=(pl.BlockSpec((n_na, tm, LANES), lambda i: (0, i, 0)),
                   pl.BlockSpec((tm, MLA_IN), lambda i: (i, 0))),
        compiler_params=_params("parallel"),
        name="even_in_proj",
    )(x, g.reshape(1, d), w)


def _na_kernel(q_ref, k_ref, v_ref, b_ref, o_ref, *, rows):
    wr = NA_ROWS
    kw = wr * GRID_W
    left = _lane_iota((GRID_W, LANES)) < HEAD_DIM
    keep = [left.astype(F32).astype(BF16), jnp.logical_not(left).astype(F32).astype(BF16)]

    def one_row(r):
        r_start = jnp.clip(r - wr // 2, 0, rows - wr)
        didx = r - r_start
        qoff = pl.multiple_of(r * GRID_W, GRID_W)
        koff = pl.multiple_of(r_start * GRID_W, GRID_W)
        q = q_ref[pl.ds(qoff, GRID_W), :]
        k = k_ref[pl.ds(koff, kw), :]
        v = v_ref[pl.ds(koff, kw), :]
        v_ones = jnp.concatenate([v, jnp.ones_like(v)], axis=-1)
        outs = []
        for hh in range(2):
            s = lax.dot_general(q * keep[hh], k, (((1,), (1,)), ((), ())),
                                preferred_element_type=F32)
            s = s + b_ref[hh, didx]
            m = jnp.max(s, axis=-1, keepdims=True)
            p = jnp.exp2(s - m)
            pv = jnp.dot(p.astype(BF16), v_ones, preferred_element_type=F32)
            outs.append(pv[:, :LANES] / pv[:, LANES:])
        o_ref[pl.ds(qoff, GRID_W), :] = jnp.where(left, outs[0], outs[1]).astype(o_ref.dtype)

    def several_rows(j, carry):
        for n in range(NA_UNROLL):
            one_row(NA_UNROLL * j + n)
        return carry

    lax.fori_loop(0, rows // NA_UNROLL, several_rows, 0)


def _na_bias_table(rpb, rows):
    assert rows >= NA_ROWS
    cols = np.arange(GRID_W)
    col_start = np.clip(cols - NA_COLS // 2, 0, GRID_W - NA_COLS)
    kc = np.arange(GRID_W)
    inside = (kc[None, :] >= col_start[:, None]) & (kc[None, :] < col_start[:, None] + NA_COLS)
    col_rel = kc[None, :] - cols[:, None] + (NA_COLS - 1)
    sel = (col_rel[:, :, None] == np.arange(2 * NA_COLS - 1)[None, None, :]) & inside[:, :, None]
    delta = np.arange(NA_ROWS)
    i = np.arange(NA_ROWS)
    row_rel = i[None, :] - delta[:, None] + (NA_ROWS - 1)
    by_row = rpb.astype(F32)[:, row_rel]
    b = jnp.einsum('hdiv,ckv->hdcik', by_row, jnp.asarray(sel, F32),
                   precision=lax.Precision.HIGHEST)
    b = b + jnp.asarray(np.where(inside, 0.0, NEG_INF), F32)[None, None, :, None, :]
    return (b * LOG2E).reshape(rpb.shape[0], NA_ROWS, GRID_W, NA_ROWS * GRID_W)


def na_attention(qkv, bias, *, batch, seq):
    n = qkv.shape[1]
    rows = seq // GRID_W
    pairs = NA_HEADS // 2
    blk = (None, seq, LANES)
    return pl.pallas_call(
        functools.partial(_na_kernel, rows=rows),
        out_shape=jax.ShapeDtypeStruct((pairs, n, LANES), BF16),
        grid=(batch, pairs),
        in_specs=[
            pl.BlockSpec(blk, lambda b, p: (p, b, 0)),
            pl.BlockSpec(blk, lambda b, p: (pairs + p, b, 0)),
            pl.BlockSpec(blk, lambda b, p: (2 * pairs + p, b, 0)),
            pl.BlockSpec((2, NA_ROWS, GRID_W, NA_ROWS * GRID_W), lambda b, p: (p, 0, 0, 0)),
        ],
        out_specs=pl.BlockSpec(blk, lambda b, p: (p, b, 0)),
        compiler_params=_params("parallel", "parallel"),
        name="na_attention",
    )(qkv, qkv, qkv, bias)


def _mla_prep_kernel(h_ref, qn_ref, kvn_ref, wq_ref, wk_ref, wv_ref, e_ref,
                     cos_ref, sin_ref, q_ref, k_ref, v_ref, *, q_scale):
    h = h_ref[...].astype(F32)
    o1 = MLA_Q_RANK
    o2 = o1 + MLA_KV_RANK
    c_q = _rms(h[:, :o1], qn_ref[...]).astype(BF16)
    c_kv = _rms(h[:, o1:o2], kvn_ref[...]).astype(BF16)
    k_pe = h_ref[:, o2:o2 + MLA_ROPE]
    q = jnp.dot(c_q, wq_ref[...], preferred_element_type=F32)
    k = (jnp.dot(c_kv, wk_ref[...], preferred_element_type=F32)
         + jnp.dot(k_pe, e_ref[...], preferred_element_type=F32))
    v = jnp.dot(c_kv, wv_ref[...], preferred_element_type=F32)
    cos = cos_ref[...]
    sin = sin_ref[...]
    for p in range(MLA_HEADS // 2):
        qs, ks = [], []
        for hd in (2 * p, 2 * p + 1):
            sl = slice(hd * MLA_PAD, (hd + 1) * MLA_PAD)
            qs.append(_rope_chunk(q[:, sl], cos, sin, MLA_ROPE // 2) * q_scale)
            ks.append(_rope_chunk(k[:, sl], cos, sin, MLA_ROPE // 2))
        q_ref[p] = jnp.concatenate(qs, axis=-1).astype(q_ref.dtype)
        k_ref[p] = jnp.concatenate(ks, axis=-1).astype(k_ref.dtype)
        v_ref[p] = v[:, 2 * p * MLA_V:(2 * p + 2) * MLA_V].astype(v_ref.dtype)


def mla_prep(h, q_norm, w_uq, kv_norm, w_ukv, cos_t, sin_t, *, seq, q_scale, tm=512):
    n = h.shape[0]
    hq = MLA_NOPE + MLA_ROPE
    pairs = MLA_HEADS // 2
    wq = w_uq.reshape(MLA_Q_RANK, MLA_HEADS, hq)
    wq = jnp.pad(wq, ((0, 0), (0, 0), (0, MLA_PAD - hq))).reshape(MLA_Q_RANK, MLA_HEADS * MLA_PAD)
    wkv = w_ukv.reshape(MLA_KV_RANK, MLA_HEADS, MLA_NOPE + MLA_V)
    wk = jnp.pad(wkv[:, :, :MLA_NOPE], ((0, 0), (0, 0), (0, MLA_PAD - MLA_NOPE)))
    wk = wk.reshape(MLA_KV_RANK, MLA_HEADS * MLA_PAD)
    wv = wkv[:, :, MLA_NOPE:].reshape(MLA_KV_RANK, MLA_HEADS * MLA_V)
    e = np.zeros((MLA_ROPE, MLA_HEADS, MLA_PAD), np.float32)
    for t in range(MLA_ROPE):
        e[t, :, MLA_NOPE + t] = 1.0
    e = jnp.asarray(e.reshape(MLA_ROPE, MLA_HEADS * MLA_PAD), BF16)
    nsb = seq // tm
    return pl.pallas_call(
        functools.partial(_mla_prep_kernel, q_scale=q_scale),
        out_shape=(jax.ShapeDtypeStruct((pairs, n, 2 * MLA_PAD), BF16),
                   jax.ShapeDtypeStruct((pairs, n, 2 * MLA_PAD), BF16),
                   jax.ShapeDtypeStruct((pairs, n, 2 * MLA_V), BF16)),
        grid=(n // tm,),
        in_specs=[
            pl.BlockSpec((tm, MLA_IN), lambda i: (i, 0)),
            _full_spec((1, MLA_Q_RANK)),
            _full_spec((1, MLA_KV_RANK)),
            _full_spec(wq.shape), _full_spec(wk.shape), _full_spec(wv.shape), _full_spec(e.shape),
            pl.BlockSpec((tm, LANES), lambda i: (i % nsb, 0)),
            pl.BlockSpec((tm, LANES), lambda i: (i % nsb, 0)),
        ],
        out_specs=(pl.BlockSpec((pairs, tm, 2 * MLA_PAD), lambda i: (0, i, 0)),
                   pl.BlockSpec((pairs, tm, 2 * MLA_PAD), lambda i: (0, i, 0)),
                   pl.BlockSpec((pairs, tm, 2 * MLA_V), lambda i: (0, i, 0))),
        compiler_params=_params("parallel"),
        name="mla_prep",
    )(h, q_norm.reshape(1, -1), kv_norm.reshape(1, -1),
      wq.astype(BF16), wk.astype(BF16), wv.astype(BF16), e, cos_t, sin_t)


def _flash_kernel(q_ref, k_ref, v_ref, o_ref, m_scr, l_scr, acc_scr, s_scr, *, tk):
    tq = q_ref.shape[0]
    nk = k_ref.shape[0] // tk
    left = _lane_iota((tq, LANES)) < MLA_V
    m_scr[...] = jnp.full((tq, LANES), NEG_INF, F32)
    l_scr[...] = jnp.zeros((tq, LANES), F32)
    acc_scr[...] = jnp.zeros((tq, LANES), F32)

    def scores(ki, slot):
        for hh in range(2):
            q = q_ref[:, hh * MLA_PAD:(hh + 1) * MLA_PAD]
            k = k_ref[ki * tk:(ki + 1) * tk, hh * MLA_PAD:(hh + 1) * MLA_PAD]
            s_scr[slot, hh] = lax.dot_general(q, k, (((1,), (1,)), ((), ())),
                                              preferred_element_type=F32)

    def consume(ki, slot):
        v = v_ref[ki * tk:(ki + 1) * tk, :]
        m_old = m_scr[...]
        m_new, p_sum, pv = [], [], []
        for hh in range(2):
            s = s_scr[slot, hh]
            mine = left if hh == 0 else jnp.logical_not(left)
            m_h = jnp.max(jnp.concatenate([s, jnp.where(mine, m_old, NEG_INF)], axis=-1),
                          axis=-1, keepdims=True)
            p = jnp.exp2(s - m_h)
            m_new.append(m_h)
            p_sum.append(jnp.sum(p, axis=-1, keepdims=True))
            pv.append(jnp.dot(p.astype(BF16), v, preferred_element_type=F32))
        m_pair = jnp.where(left, m_new[0], m_new[1])
        alpha = jnp.exp2(m_old - m_pair)
        m_scr[...] = m_pair
        l_scr[...] = alpha * l_scr[...] + jnp.where(left, p_sum[0], p_sum[1])
        acc_scr[...] = alpha * acc_scr[...] + jnp.where(left, pv[0], pv[1])

    scores(0, 0)
    for ki in range(nk):
        if ki + 1 < nk:
            scores(ki + 1, (ki + 1) % 2)
        consume(ki, ki % 2)
    o_ref[...] = (acc_scr[...] / l_scr[...]).astype(o_ref.dtype)


def flash_attention(q, k, v, *, batch, seq, tq=256, tk=1024):
    pairs, n, _ = q.shape
    dv = MLA_V
    tq = min(tq, seq)
    tk = min(tk, seq)
    nq = seq // tq
    return pl.pallas_call(
        functools.partial(_flash_kernel, tk=tk),
        out_shape=jax.ShapeDtypeStruct((pairs, n, 2 * dv), BF16),
        grid=(batch, pairs, nq),
        in_specs=[
            pl.BlockSpec((None, tq, 2 * MLA_PAD), lambda b, p, i: (p, b * nq + i, 0)),
            pl.BlockSpec((None, seq, 2 * MLA_PAD), lambda b, p, i: (p, b, 0)),
            pl.BlockSpec((None, seq, 2 * dv), lambda b, p, i: (p, b, 0)),
        ],
        out_specs=pl.BlockSpec((None, tq, 2 * dv), lambda b, p, i: (p, b * nq + i, 0)),
        scratch_shapes=[pltpu.VMEM((tq, LANES), F32)] * 3 + [pltpu.VMEM((2, 2, tq, tk), F32)],
        compiler_params=_params("parallel", "parallel", "arbitrary"),
        name="flash_attention",
    )(q, k, v)


def _row_permutation(tm, d):
    p = np.zeros((tm, tm), np.float32)
    rho = np.arange(tm)
    r, a = rho // (tm // d), rho % (tm // d)
    p[rho, a * d + r] = 1.0
    return jnp.asarray(p, BF16)


def _dilated_in_proj_kernel(*refs, d):
    if d > 1:
        x_ref, g_ref, w_ref, cos_ref, sin_ref, perm_ref, o_ref, xn_ref = refs
    else:
        x_ref, g_ref, w_ref, cos_ref, sin_ref, o_ref, xn_ref = refs
    tm = x_ref.shape[0]

    @pl.when(pl.program_id(1) == 0)
    def _():
        xn = _rms(x_ref[...], g_ref[...]).astype(BF16)
        if d > 1:
            xn = jnp.dot(perm_ref[...], xn, preferred_element_type=F32).astype(BF16)
        xn_ref[...] = xn

    cos = cos_ref[...]
    sin = sin_ref[...]
    rows = tm // d
    lead = o_ref.shape[1]
    per = rows // lead
    hw = o_ref.shape[-1]
    for hg in range(DIL_HGRP):
        acc = jnp.dot(xn_ref[...], w_ref[:, hg * hw:(hg + 1) * hw], preferred_element_type=F32)
        parts = []
        for c in range(hw // LANES):
            x = acc[:, c * LANES:(c + 1) * LANES]
            parts.append((x * cos + pltpu.roll(x, LANES // 2, 1) * sin).astype(o_ref.dtype))
        y = jnp.concatenate(parts, axis=-1)
        for t in range(lead):
            for r in range(d):
                lo = r * rows + t * per
                o_ref[hg, t, r] = y[lo:lo + per]


def dilated_in_proj(x, g, w, cos_t, sin_t, *, group, seq, tm=512):
    n, dm = x.shape
    d = DIL_PAIRS[group][1]
    width = DIL_HEADS * HEAD_DIM
    pt = DIL_QBLK * d
    hw = width // DIL_HGRP
    nsb = seq // tm
    in_specs = [
        pl.BlockSpec((tm, dm), lambda i, k: (i, 0)),
        _full_spec((1, dm)),
        pl.BlockSpec((None, dm, width), lambda i, k: (3 * group + k, 0, 0)),
        pl.BlockSpec((None, tm, LANES), lambda i, k: (k, i % nsb, 0)),
        pl.BlockSpec((None, tm, LANES), lambda i, k: (k, i % nsb, 0)),
    ]
    args = [x, g.reshape(1, dm), w, cos_t, sin_t]
    if d > 1:
        in_specs.append(_full_spec((tm, tm)))
        args.append(_row_permutation(tm, d))
    if pt >= tm:
        sub = pt // tm
        out_spec = pl.BlockSpec((None, DIL_HGRP, 1, d, tm // d, hw),
                                lambda i, k: (k, 0, i // sub, 0, i % sub, 0))
    else:
        lead = tm // pt
        out_spec = pl.BlockSpec((None, DIL_HGRP, lead, d, DIL_QBLK, hw),
                                lambda i, k: (k, 0, i, 0, 0, 0))
    return pl.pallas_call(
        functools.partial(_dilated_in_proj_kernel, d=d),
        out_shape=jax.ShapeDtypeStruct((3, DIL_HGRP, n // pt, d, DIL_QBLK, hw), BF16),
        grid=(n // tm, 3),
        in_specs=in_specs,
        out_specs=out_spec,
        scratch_shapes=[pltpu.VMEM((tm, dm), BF16)],
        compiler_params=_params("parallel", "arbitrary"),
        name=f"dilated_in_proj{group}",
    )(*args)


def _dilated_weight_slabs(w):
    dm = w.shape[0]
    width = DIL_HEADS * HEAD_DIM
    half = HEAD_DIM // 2
    ng = len(DIL_PAIRS)
    w = w.astype(BF16).reshape(dm, ng, 3, width)
    qk = w[:, :, :2].reshape(dm, ng, 2, width // LANES, 2, 2, half)
    qk = qk.transpose(0, 1, 2, 3, 5, 4, 6).reshape(dm, ng, 2, width)
    w = jnp.concatenate([qk, w[:, :, 2:]], axis=2)
    return w.reshape(dm, 3 * ng, width).transpose(1, 0, 2)


def _dilated_rope_tables(seq, q_scale, d, tm):
    half = HEAD_DIM // 2
    inv = 1.0 / (ROPE_THETA ** (jnp.arange(0, HEAD_DIM, 2, dtype=F32) / HEAD_DIM))
    ang = jnp.arange(seq, dtype=F32)[:, None] * inv[None, :]
    lane = np.arange(LANES)
    cos = jnp.cos(ang)[:, lane % half]
    sin = jnp.sin(ang)[:, lane % half] * jnp.asarray(np.where(lane < LANES // 2, -1.0, 1.0), F32)
    cos = jnp.stack([cos * q_scale, cos, jnp.ones_like(cos)])
    sin = jnp.stack([sin * q_scale, sin, jnp.zeros_like(sin)])

    def tile_order(t):
        t = t.reshape(3, seq // tm, tm // d, d, LANES).transpose(0, 1, 3, 2, 4)
        return t.reshape(3, seq, LANES)

    return tile_order(cos), tile_order(sin)


def _dilated_kernel(*refs, seq):
    ng = len(DIL_PAIRS)
    ins = [refs[7 * g:7 * g + 7] for g in range(ng)]
    out_ref, acc_scr, m_scr, l_scr = refs[7 * ng:]
    i = pl.program_id(1)
    n_pairs = DIL_HGRP // 2
    left = _lane_iota((DIL_QBLK, LANES)) < HEAD_DIM
    q_first = (_lane_iota((DIL_QBLK, LANES)) % HEAD_DIM) < HEAD_DIM // 2
    keep = [q_first.astype(F32).astype(BF16), jnp.logical_not(q_first).astype(F32).astype(BF16)]
    jq = lax.broadcasted_iota(jnp.int32, (DIL_QBLK, DIL_KBLK), 0)
    jk = lax.broadcasted_iota(jnp.int32, (DIL_QBLK, DIL_KBLK), 1)
    rel = jk - jq
    band = jnp.where((rel >= 0) & (rel <= 2 * DIL_HALF), 0.0, NEG_INF).astype(F32)
    jk_row = lax.broadcasted_iota(jnp.int32, (1, DIL_KBLK), 1)

    for step, g in enumerate(DIL_ORDER):
        d = DIL_PAIRS[g][1]
        q_ref, kp_ref, kc_ref, kn_ref, vp_ref, vc_ref, vn_ref = ins[g]
        nt = kc_ref.shape[0]
        sub_len = seq // d
        first = step == 0
        last = step == ng - 1

        def window(p_ref, c_ref, n_ref, t, r, nt=nt):
            if nt == 1:
                before, after = p_ref[r], n_ref[r]
            else:
                before = jnp.where(t == 0, p_ref[r],
                                   c_ref[jnp.maximum(t - 1, 0), r, DIL_HALF:, :])
                after = jnp.where(t == nt - 1, n_ref[r],
                                  c_ref[jnp.minimum(t + 1, nt - 1), r, :DIL_HALF, :])
            return jnp.concatenate([before, c_ref[t, r], after], axis=0)

        def load(idx, d=d, sub_len=sub_len, first=first, q_ref=q_ref, kp_ref=kp_ref,
                 kc_ref=kc_ref, kn_ref=kn_ref, vp_ref=vp_ref, vc_ref=vc_ref, vn_ref=vn_ref,
                 window=window):
            t = idx // d
            r = idx % d
            base = i * (DIL_TOK // d) + t * DIL_QBLK - DIL_HALF
            edge = jnp.where((jk_row >= -base) & (jk_row < sub_len - base), 0.0, NEG_INF)
            start = t * (DIL_QBLK * d) + r
            rows = pl.ds(start, DIL_QBLK, stride=d) if d > 1 else pl.ds(start, DIL_QBLK)
            state = None
            if not first:
                state = [(acc_scr[hp, rows, :], m_scr[hp, rows, :], l_scr[hp, rows, :])
                         for hp in range(n_pairs)]
            return dict(q=q_ref[t, r], k=window(kp_ref, kc_ref, kn_ref, t, r),
                        v=window(vp_ref, vc_ref, vn_ref, t, r),
                        bias=band + edge.astype(F32), rows=rows, state=state)

        def compute(u, first=first, last=last):
            res = []
            for hp in range(n_pairs):
                ps = slice(hp * LANES, (hp + 1) * LANES)
                q, k, v = u["q"][:, ps], u["k"][:, ps], u["v"][:, ps]
                if not first:
                    acc_old, m_old, l_old = u["state"][hp]
                v_ones = jnp.concatenate([v, jnp.ones_like(v)], axis=-1)
                m_h, pv = [], []
                for hh in range(2):
                    s = lax.dot_general(q * keep[hh], k, (((1,), (1,)), ((), ())),
                                        preferred_element_type=F32) + u["bias"]
                    if first:
                        m = jnp.max(s, axis=-1, keepdims=True)
                    else:
                        mine = left if hh == 0 else jnp.logical_not(left)
                        m = jnp.max(jnp.concatenate([s, jnp.where(mine, m_old, NEG_INF)], axis=-1),
                                    axis=-1, keepdims=True)
                    p = jnp.exp2(s - m)
                    m_h.append(m)
                    pv.append(jnp.dot(p.astype(BF16), v_ones, preferred_element_type=F32))
                m_new = jnp.where(left, m_h[0], m_h[1])
                l_new = jnp.where(left, pv[0][:, LANES:], pv[1][:, LANES:])
                acc = jnp.where(left, pv[0][:, :LANES], pv[1][:, :LANES])
                if not first:
                    alpha = jnp.exp2(m_old - m_new)
                    l_new = l_new + alpha * l_old
                    acc = acc + alpha * acc_old
                if last:
                    acc = acc / l_new
                res.append((acc, m_new, l_new))
            return res

        def store(u, res, last=last):
            rows = u["rows"]
            for hp, (acc, m_new, l_new) in enumerate(res):
                if last:
                    out_ref[rows, hp * LANES:(hp + 1) * LANES] = acc.astype(out_ref.dtype)
                else:
                    acc_scr[hp, rows, :] = acc
                    m_scr[hp, rows, :] = m_new
                    l_scr[hp, rows, :] = l_new

        def several_units(j, carry, load=load, compute=compute, store=store):
            units = [load(DIL_UNROLL * j + n) for n in range(DIL_UNROLL)]
            results = [compute(u) for u in units]
            for u, res in zip(units, results):
                store(u, res)
            return carry

        lax.fori_loop(0, nt * d // DIL_UNROLL, several_units, 0)


def dilated_attention(hq, *, batch, seq):
    assert seq % DIL_TOK == 0
    n = batch * seq
    hw = DIL_HEADS * HEAD_DIM // DIL_HGRP
    nblk = seq // DIL_TOK
    in_specs, args = [], []
    for g, (_, d) in enumerate(DIL_PAIRS):
        pt = DIL_QBLK * d
        nt = DIL_TOK // pt
        tiles = seq // pt

        def cur(kind, nt=nt, d=d):
            return pl.BlockSpec((None, None, nt, d, DIL_QBLK, hw),
                                lambda b, i, hg: (kind, hg, b * nblk + i, 0, 0, 0))

        def prev(kind, nt=nt, d=d, tiles=tiles):
            return pl.BlockSpec(
                (None, None, None, d, DIL_HALF, hw),
                lambda b, i, hg: (kind, hg, b * tiles + jnp.maximum(i * nt - 1, 0), 0, 1, 0))

        def nxt(kind, nt=nt, d=d, tiles=tiles):
            return pl.BlockSpec(
                (None, None, None, d, DIL_HALF, hw),
                lambda b, i, hg: (kind, hg, b * tiles + jnp.minimum((i + 1) * nt, tiles - 1), 0, 0, 0))

        in_specs += [cur(0), prev(1), cur(1), nxt(1), prev(2), cur(2), nxt(2)]
        args += [hq[g]] * 7
    return pl.pallas_call(
        functools.partial(_dilated_kernel, seq=seq),
        out_shape=jax.ShapeDtypeStruct((DIL_HGRP, n, hw), BF16),
        grid=(batch, nblk, DIL_HGRP),
        in_specs=in_specs,
        out_specs=pl.BlockSpec((None, DIL_TOK, hw), lambda b, i, hg: (hg, b * nblk + i, 0)),
        scratch_shapes=[pltpu.VMEM((DIL_HGRP // 2, DIL_TOK, LANES), F32)] * 3,
        compiler_params=_params("parallel", "parallel", "arbitrary"),
        name="dilated_attention",
    )(*args)


def _proj_residual_kernel(*refs, n_in):
    a_refs = refs[:n_in]
    w_ref, x_ref, o_ref = refs[n_in:]
    pieces = [a[s] for a in a_refs for s in range(a.shape[0])]
    a = jnp.concatenate(pieces, axis=-1)
    o_ref[...] = x_ref[...] + jnp.dot(a, w_ref[...], preferred_element_type=F32)


def proj_residual(parts, w, x, *, tm=512):
    n, d = x.shape
    in_specs = ([pl.BlockSpec((a.shape[0], tm, a.shape[2]), lambda i: (0, i, 0)) for a in parts]
                + [_full_spec(w.shape), pl.BlockSpec((tm, d), lambda i: (i, 0))])
    return pl.pallas_call(
        functools.partial(_proj_residual_kernel, n_in=len(parts)),
        out_shape=jax.ShapeDtypeStruct((n, d), F32),
        grid=(n // tm,),
        in_specs=in_specs,
        out_specs=pl.BlockSpec((tm, d), lambda i: (i, 0)),
        compiler_params=_params("parallel"),
        name="proj_residual",
    )(*parts, w, x)


def _mlp_kernel(x_ref, g_ref, w1_ref, w2_ref, gf_ref, o_ref, xn_ref, acc_ref, *, final_norm):
    j = pl.program_id(1)

    @pl.when(j == 0)
    def _():
        xn_ref[...] = _rms(x_ref[...], g_ref[...]).astype(BF16)
        acc_ref[...] = jnp.zeros_like(acc_ref)

    h = jnp.dot(xn_ref[...], w1_ref[...], preferred_element_type=F32)
    h = jnp.square(jnp.maximum(h, 0.0)).astype(BF16)
    acc_ref[...] += jnp.dot(h, w2_ref[...], preferred_element_type=F32)

    @pl.when(j == pl.num_programs(1) - 1)
    def _():
        y = x_ref[...] + acc_ref[...]
        if final_norm:
            y = _rms(y, gf_ref[...])
        o_ref[...] = y


def mlp(x, g, w1, w2, g_final, *, final_norm, tm=512):
    n, d = x.shape
    nf, _, tf = w1.shape
    return pl.pallas_call(
        functools.partial(_mlp_kernel, final_norm=final_norm),
        out_shape=jax.ShapeDtypeStruct((n, d), F32),
        grid=(n // tm, nf),
        in_specs=[
            pl.BlockSpec((tm, d), lambda i, j: (i, 0)),
            _full_spec((1, d)),
            pl.BlockSpec((None, d, tf), lambda i, j: (j, 0, 0)),
            pl.BlockSpec((tf, d), lambda i, j: (j, 0)),
            _full_spec((1, d)),
        ],
        out_specs=pl.BlockSpec((tm, d), lambda i, j: (i, 0)),
        scratch_shapes=[pltpu.VMEM((tm, d), BF16), pltpu.VMEM((tm, d), F32)],
        compiler_params=_params("parallel", "arbitrary"),
        name="mlp",
    )(x, g.reshape(1, d), w1, w2, g_final.reshape(1, d))


def _column_slabs(w, tf):
    d, f = w.shape
    return w.astype(BF16).reshape(d, f // tf, tf).transpose(1, 0, 2)


def _rope_lane_tables(seq, dim, lane_lo, lane_hi, period):
    half = dim // 2
    inv = 1.0 / (ROPE_THETA ** (jnp.arange(0, dim, 2, dtype=F32) / dim))
    ang = jnp.arange(seq, dtype=F32)[:, None] * inv[None, :]
    cos, sin = jnp.cos(ang), jnp.sin(ang)
    lane = np.arange(LANES)
    pos = lane % period
    active = (pos >= lane_lo) & (pos < lane_hi)
    idx = (pos - lane_lo) % half
    second = ((pos - lane_lo) % dim) >= half
    cos_t = jnp.where(active[None, :], cos[:, idx], 1.0)
    sin_t = jnp.where(active[None, :], jnp.where(second[None, :], sin[:, idx], -sin[:, idx]), 0.0)
    return cos_t.astype(F32), sin_t.astype(F32)


def kernel(x, norm_mix, norm_mlp, norm_final, ev_w_in, ev_rpb, ev_q_norm, ev_w_uq,
           ev_kv_norm, ev_w_ukv, ev_w_o, od_w_in, od_w_o, mlp_w1, mlp_w2):
    batch, seq, d = x.shape
    n = batch * seq
    depth = norm_mix.shape[0]
    rows = seq // GRID_W
    width = DIL_HEADS * HEAD_DIM

    dil_q_scale = HEAD_DIM ** -0.5 * LOG2E
    dil_tables = [_dilated_rope_tables(seq, dil_q_scale, d, 512) for _, d in DIL_PAIRS]
    cos_r, sin_r = _rope_lane_tables(seq, MLA_ROPE, MLA_NOPE, MLA_NOPE + MLA_ROPE, MLA_PAD)
    mla_q_scale = (MLA_NOPE + MLA_ROPE) ** -0.5 * LOG2E
    even_width = 3 * NA_WIDTH + MLA_IN

    xs = x.reshape(n, d)
    for layer in range(depth):
        idx = layer // 2
        if layer % 2 == 0:
            w_in = jnp.pad(ev_w_in[idx], ((0, 0), (0, even_width - ev_w_in.shape[2]))).astype(BF16)
            na_qkv, latent = even_in_proj(xs, norm_mix[layer], w_in)
            bias = _na_bias_table(ev_rpb[idx], rows)
            out_a = na_attention(na_qkv, bias, batch=batch, seq=seq)
            q, k, v = mla_prep(latent, ev_q_norm[idx], ev_w_uq[idx], ev_kv_norm[idx],
                               ev_w_ukv[idx], cos_r, sin_r, seq=seq, q_scale=mla_q_scale)
            out_b = flash_attention(q, k, v, batch=batch, seq=seq)
            xs = proj_residual([out_a, out_b], ev_w_o[idx].astype(BF16), xs)
        else:
            w_in = _dilated_weight_slabs(od_w_in[idx])
            hq = [dilated_in_proj(xs, norm_mix[layer], w_in, *dil_tables[g], group=g, seq=seq)
                  for g in range(len(DIL_PAIRS))]
            out = dilated_attention(hq, batch=batch, seq=seq)
            xs = proj_residual([out], od_w_o[idx].astype(BF16), xs)
        xs = mlp(xs, norm_mlp[layer], _column_slabs(mlp_w1[layer], 1024),
                 mlp_w2[layer].astype(BF16), norm_final, final_norm=(layer == depth - 1))
    return xs.reshape(batch, seq, d)
```

```python
import functools
import math

import numpy as np
import jax
import jax.numpy as jnp
from jax import lax
from jax.experimental import pallas as pl
from jax.experimental.pallas import tpu as pltpu

F32 = jnp.float32
BF16 = jnp.bfloat16

D_MODEL = 1024
HEAD_DIM = 64
ROPE_THETA = 10000.0
RMS_EPS = 1e-6
NEG_INF = -1e30
LOG2E = math.log2(math.e)

GRID_W = 64
NA_HEADS = 8
NA_ROWS = 8
NA_COLS = 16
NA_WIDTH = NA_HEADS * HEAD_DIM
NA_UNROLL = 8

MLA_HEADS = 8
MLA_Q_RANK = 256
MLA_KV_RANK = 128
MLA_NOPE = 64
MLA_ROPE = 32
MLA_V = 64
MLA_PAD = 128
MLA_IN = 512

DIL_PAIRS = ((128, 1), (512, 4), (2048, 16))
DIL_HEADS = D_MODEL // HEAD_DIM
DIL_HALF = 64
DIL_QBLK = 128
DIL_KBLK = 256
DIL_HGRP = 4
DIL_TOK = DIL_QBLK * max(d for _, d in DIL_PAIRS)
DIL_ORDER = (2, 1, 0)
DIL_UNROLL = 4
DIL_PERM = 512

D_FF = 4 * D_MODEL

LANES = 128
VMEM_LIMIT_BYTES = 56 * 1024 * 1024


def _params(*semantics):
    return pltpu.CompilerParams(dimension_semantics=semantics,
                                vmem_limit_bytes=VMEM_LIMIT_BYTES)


def _rms(x, g):
    ms = jnp.mean(x * x, axis=-1, keepdims=True)
    return x * lax.rsqrt(ms + RMS_EPS) * g


def _lane_iota(shape):
    return lax.broadcasted_iota(jnp.int32, shape, len(shape) - 1)


def _rope_chunk(x, cos, sin_signed, half):
    lane = _lane_iota(x.shape)
    fwd = pltpu.roll(x, LANES - half, 1)
    bwd = pltpu.roll(x, half, 1)
    rot = jnp.where((lane % (2 * half)) < half, fwd, bwd)
    return x * cos + rot * sin_signed


def _full_spec(shape):
    return pl.BlockSpec(shape, lambda *_: (0,) * len(shape))


def _even_in_proj_kernel(x_ref, g_ref, w_ref, na_ref, h_ref):
    xn = _rms(x_ref[...], g_ref[...]).astype(BF16)
    acc = jnp.dot(xn, w_ref[...], preferred_element_type=F32)
    n_na = na_ref.shape[0]
    q_slabs = n_na // 3
    for c in range(n_na):
        y = acc[:, c * LANES:(c + 1) * LANES]
        if c < q_slabs:
            y = y * (HEAD_DIM ** -0.5 * LOG2E)
        na_ref[c] = y.astype(na_ref.dtype)
    h_ref[...] = acc[:, n_na * LANES:].astype(h_ref.dtype)


def even_in_proj(x, g, w, *, tm=512):
    n, d = x.shape
    f = w.shape[1]
    n_na = 3 * NA_WIDTH // LANES
    assert f == n_na * LANES + MLA_IN
    return pl.pallas_call(
        _even_in_proj_kernel,
        out_shape=(jax.ShapeDtypeStruct((n_na, n, LANES), BF16),
                   jax.ShapeDtypeStruct((n, MLA_IN), BF16)),
        grid=(n // tm,),
        in_specs=[pl.BlockSpec((tm, d), lambda i: (i, 0)),
                  _full_spec((1, d)), _full_spec((d, f))],
        out_specs=(pl.BlockSpec((n_na, tm, LANES), lambda i: (0, i, 0)),
                   pl.BlockSpec((tm, MLA_IN), lambda i: (i, 0))),
        compiler_params=_params("parallel"),
        name="even_in_proj",
    )(x, g.reshape(1, d), w)


def _na_kernel(q_ref, k_ref, v_ref, b_ref, o_ref, *, rows):
    wr = NA_ROWS
    kw = wr * GRID_W
    left = _lane_iota((GRID_W, LANES)) < HEAD_DIM
    keep = [left.astype(F32).astype(BF16), jnp.logical_not(left).astype(F32).astype(BF16)]

    def one_row(r):
        r_start = jnp.clip(r - wr // 2, 0, rows - wr)
        didx = r - r_start
        qoff = pl.multiple_of(r * GRID_W, GRID_W)
        koff = pl.multiple_of(r_start * GRID_W, GRID_W)
        q = q_ref[pl.ds(qoff, GRID_W), :]
        k = k_ref[pl.ds(koff, kw), :]
        v = v_ref[pl.ds(koff, kw), :]
        v_ones = jnp.concatenate([v, jnp.ones_like(v)], axis=-1)
        outs = []
        for hh in range(2):
            s = lax.dot_general(q * keep[hh], k, (((1,), (1,)), ((), ())),
                                preferred_element_type=F32)
            s = s + b_ref[hh, didx]
            m = jnp.max(s, axis=-1, keepdims=True)
            p = jnp.exp2(s - m)
            pv = jnp.dot(p.astype(BF16), v_ones, preferred_element_type=F32)
            outs.append(pv[:, :LANES] / pv[:, LANES:])
        o_ref[pl.ds(qoff, GRID_W), :] = jnp.where(left, outs[0], outs[1]).astype(o_ref.dtype)

    def several_rows(j, carry):
        for n in range(NA_UNROLL):
            one_row(NA_UNROLL * j + n)
        return carry

    lax.fori_loop(0, rows // NA_UNROLL, several_rows, 0)


def _na_bias_table(rpb, rows):
    assert rows >= NA_ROWS
    cols = np.arange(GRID_W)
    col_start = np.clip(cols - NA_COLS // 2, 0, GRID_W - NA_COLS)
    kc = np.arange(GRID_W)
    inside = (kc[None, :] >= col_start[:, None]) & (kc[None, :] < col_start[:, None] + NA_COLS)
    col_rel = kc[None, :] - cols[:, None] + (NA_COLS - 1)
    sel = (col_rel[:, :, None] == np.arange(2 * NA_COLS - 1)[None, None, :]) & inside[:, :, None]
    delta = np.arange(NA_ROWS)
    i = np.arange(NA_ROWS)
    row_rel = i[None, :] - delta[:, None] + (NA_ROWS - 1)
    by_row = rpb.astype(F32)[:, row_rel]
    b = jnp.einsum('hdiv,ckv->hdcik', by_row, jnp.asarray(sel, F32),
                   precision=lax.Precision.HIGHEST)
    b = b + jnp.asarray(np.where(inside, 0.0, NEG_INF), F32)[None, None, :, None, :]
    return (b * LOG2E).reshape(rpb.shape[0], NA_ROWS, GRID_W, NA_ROWS * GRID_W)


def na_attention(qkv, bias, *, batch, seq):
    n = qkv.shape[1]
    rows = seq // GRID_W
    pairs = NA_HEADS // 2
    blk = (None, seq, LANES)
    return pl.pallas_call(
        functools.partial(_na_kernel, rows=rows),
        out_shape=jax.ShapeDtypeStruct((pairs, n, LANES), BF16),
        grid=(batch, pairs),
        in_specs=[
            pl.BlockSpec(blk, lambda b, p: (p, b, 0)),
            pl.BlockSpec(blk, lambda b, p: (pairs + p, b, 0)),
            pl.BlockSpec(blk, lambda b, p: (2 * pairs + p, b, 0)),
            pl.BlockSpec((2, NA_ROWS, GRID_W, NA_ROWS * GRID_W), lambda b, p: (p, 0, 0, 0)),
        ],
        out_specs=pl.BlockSpec(blk, lambda b, p: (p, b, 0)),
        compiler_params=_params("parallel", "parallel"),
        name="na_attention",
    )(qkv, qkv, qkv, bias)


def _mla_prep_kernel(h_ref, qn_ref, kvn_ref, wq_ref, wk_ref, wv_ref, e_ref,
                     cos_ref, sin_ref, q_ref, k_ref, v_ref, *, q_scale):
    h = h_ref[...].astype(F32)
    o1 = MLA_Q_RANK
    o2 = o1 + MLA_KV_RANK
    c_q = _rms(h[:, :o1], qn_ref[...]).astype(BF16)
    c_kv = _rms(h[:, o1:o2], kvn_ref[...]).astype(BF16)
    k_pe = h_ref[:, o2:o2 + MLA_ROPE]
    q = jnp.dot(c_q, wq_ref[...], preferred_element_type=F32)
    k = (jnp.dot(c_kv, wk_ref[...], preferred_element_type=F32)
         + jnp.dot(k_pe, e_ref[...], preferred_element_type=F32))
    v = jnp.dot(c_kv, wv_ref[...], preferred_element_type=F32)
    cos = cos_ref[...]
    sin = sin_ref[...]
    for p in range(MLA_HEADS // 2):
        qs, ks = [], []
        for hd in (2 * p, 2 * p + 1):
            sl = slice(hd * MLA_PAD, (hd + 1) * MLA_PAD)
            qs.append(_rope_chunk(q[:, sl], cos, sin, MLA_ROPE // 2) * q_scale)
            ks.append(_rope_chunk(k[:, sl], cos, sin, MLA_ROPE // 2))
        q_ref[p] = jnp.concatenate(qs, axis=-1).astype(q_ref.dtype)
        k_ref[p] = jnp.concatenate(ks, axis=-1).astype(k_ref.dtype)
        v_pair = v[:, 2 * p * MLA_V:(2 * p + 2) * MLA_V]
        v_ref[p] = jnp.concatenate([v_pair, jnp.ones_like(v_pair)], axis=-1).astype(v_ref.dtype)


def mla_prep(h, q_norm, w_uq, kv_norm, w_ukv, cos_t, sin_t, *, seq, q_scale, tm=512):
    n = h.shape[0]
    hq = MLA_NOPE + MLA_ROPE
    pairs = MLA_HEADS // 2
    wq = w_uq.reshape(MLA_Q_RANK, MLA_HEADS, hq)
    wq = jnp.pad(wq, ((0, 0), (0, 0), (0, MLA_PAD - hq))).reshape(MLA_Q_RANK, MLA_HEADS * MLA_PAD)
    wkv = w_ukv.reshape(MLA_KV_RANK, MLA_HEADS, MLA_NOPE + MLA_V)
    wk = jnp.pad(wkv[:, :, :MLA_NOPE], ((0, 0), (0, 0), (0, MLA_PAD - MLA_NOPE)))
    wk = wk.reshape(MLA_KV_RANK, MLA_HEADS * MLA_PAD)
    wv = wkv[:, :, MLA_NOPE:].reshape(MLA_KV_RANK, MLA_HEADS * MLA_V)
    e = np.zeros((MLA_ROPE, MLA_HEADS, MLA_PAD), np.float32)
    for t in range(MLA_ROPE):
        e[t, :, MLA_NOPE + t] = 1.0
    e = jnp.asarray(e.reshape(MLA_ROPE, MLA_HEADS * MLA_PAD), BF16)
    nsb = seq // tm
    return pl.pallas_call(
        functools.partial(_mla_prep_kernel, q_scale=q_scale),
        out_shape=(jax.ShapeDtypeStruct((pairs, n, 2 * MLA_PAD), BF16),
                   jax.ShapeDtypeStruct((pairs, n, 2 * MLA_PAD), BF16),
                   jax.ShapeDtypeStruct((pairs, n, 2 * MLA_V + LANES), BF16)),
        grid=(n // tm,),
        in_specs=[
            pl.BlockSpec((tm, MLA_IN), lambda i: (i, 0)),
            _full_spec((1, MLA_Q_RANK)),
            _full_spec((1, MLA_KV_RANK)),
            _full_spec(wq.shape), _full_spec(wk.shape), _full_spec(wv.shape), _full_spec(e.shape),
            pl.BlockSpec((tm, LANES), lambda i: (i % nsb, 0)),
            pl.BlockSpec((tm, LANES), lambda i: (i % nsb, 0)),
        ],
        out_specs=(pl.BlockSpec((pairs, tm, 2 * MLA_PAD), lambda i: (0, i, 0)),
                   pl.BlockSpec((pairs, tm, 2 * MLA_PAD), lambda i: (0, i, 0)),
                   pl.BlockSpec((pairs, tm, 2 * MLA_V + LANES), lambda i: (0, i, 0))),
        compiler_params=_params("parallel"),
        name="mla_prep",
    )(h, q_norm.reshape(1, -1), kv_norm.reshape(1, -1),
      wq.astype(BF16), wk.astype(BF16), wv.astype(BF16), e, cos_t, sin_t)


def _flash_kernel(q_ref, k_ref, v_ref, o_ref, m_scr, l_scr, acc_scr, s_scr, p_scr, a_scr, *, tk):
    tq = q_ref.shape[0]
    nk = k_ref.shape[0] // tk
    left = _lane_iota((tq, LANES)) < MLA_V
    m_scr[...] = jnp.full((tq, LANES), NEG_INF, F32)
    l_scr[...] = jnp.zeros((tq, LANES), F32)
    acc_scr[...] = jnp.zeros((tq, LANES), F32)

    def scores(ki):
        for hh in range(2):
            q = q_ref[:, hh * MLA_PAD:(hh + 1) * MLA_PAD]
            k = k_ref[ki * tk:(ki + 1) * tk, hh * MLA_PAD:(hh + 1) * MLA_PAD]
            s_scr[ki % 2, hh] = lax.dot_general(q, k, (((1,), (1,)), ((), ())),
                                                preferred_element_type=F32)

    def softmax(ki):
        m_old = m_scr[...]
        m_new = []
        for hh in range(2):
            s = s_scr[ki % 2, hh]
            mine = left if hh == 0 else jnp.logical_not(left)
            m_h = jnp.max(jnp.concatenate([s, jnp.where(mine, m_old, NEG_INF)], axis=-1),
                          axis=-1, keepdims=True)
            p_scr[ki % 2, hh] = jnp.exp2(s - m_h).astype(BF16)
            m_new.append(m_h)
        m_pair = jnp.where(left, m_new[0], m_new[1])
        a_scr[ki % 2] = jnp.exp2(m_old - m_pair)
        m_scr[...] = m_pair

    def values(ki):
        v = v_ref[ki * tk:(ki + 1) * tk, :]
        pv = [jnp.dot(p_scr[ki % 2, hh], v, preferred_element_type=F32) for hh in range(2)]
        alpha = a_scr[ki % 2]
        l_scr[...] = alpha * l_scr[...] + jnp.where(left, pv[0][:, LANES:], pv[1][:, LANES:])
        acc_scr[...] = alpha * acc_scr[...] + jnp.where(left, pv[0][:, :LANES], pv[1][:, :LANES])

    scores(0)
    if nk > 1:
        scores(1)
    softmax(0)
    for ki in range(nk):
        if ki + 2 < nk:
            scores(ki + 2)
        if ki + 1 < nk:
            softmax(ki + 1)
        values(ki)
    o_ref[...] = (acc_scr[...] / l_scr[...]).astype(o_ref.dtype)


def flash_attention(q, k, v, *, batch, seq, tq=256, tk=1024):
    pairs, n, _ = q.shape
    dv = MLA_V
    tq = min(tq, seq)
    tk = min(tk, seq)
    nq = seq // tq
    return pl.pallas_call(
        functools.partial(_flash_kernel, tk=tk),
        out_shape=jax.ShapeDtypeStruct((pairs, n, 2 * dv), BF16),
        grid=(batch, pairs, nq),
        in_specs=[
            pl.BlockSpec((None, tq, 2 * MLA_PAD), lambda b, p, i: (p, b * nq + i, 0)),
            pl.BlockSpec((None, seq, 2 * MLA_PAD), lambda b, p, i: (p, b, 0)),
            pl.BlockSpec((None, seq, 2 * dv + LANES), lambda b, p, i: (p, b, 0)),
        ],
        out_specs=pl.BlockSpec((None, tq, 2 * dv), lambda b, p, i: (p, b * nq + i, 0)),
        scratch_shapes=([pltpu.VMEM((tq, LANES), F32)] * 3
                        + [pltpu.VMEM((2, 2, tq, tk), F32), pltpu.VMEM((2, 2, tq, tk), BF16),
                           pltpu.VMEM((2, tq, LANES), F32)]),
        compiler_params=_params("parallel", "parallel", "arbitrary"),
        name="flash_attention",
    )(q, k, v)


def _row_permutation(tm, d):
    p = np.zeros((tm, tm), np.float32)
    rho = np.arange(tm)
    r, a = rho // (tm // d), rho % (tm // d)
    p[rho, a * d + r] = 1.0
    return jnp.asarray(p, BF16)


def _dilated_in_proj_kernel(*refs, d):
    if d > 1:
        x_ref, g_ref, w_ref, cos_ref, sin_ref, perm_ref, o_ref, xn_ref = refs
    else:
        x_ref, g_ref, w_ref, cos_ref, sin_ref, o_ref, xn_ref = refs
    tm = x_ref.shape[0]

    ps = perm_ref.shape[0] if d > 1 else tm

    @pl.when(pl.program_id(1) == 0)
    def _():
        xn = _rms(x_ref[...], g_ref[...]).astype(BF16)
        if d > 1:
            xn = jnp.concatenate(
                [jnp.dot(perm_ref[...], xn[s * ps:(s + 1) * ps], preferred_element_type=F32)
                 for s in range(tm // ps)], axis=0).astype(BF16)
        xn_ref[...] = xn

    cos = cos_ref[...]
    sin = sin_ref[...]
    lead, _, per = o_ref.shape[1:4]
    hw = o_ref.shape[-1]
    for hg in range(DIL_HGRP):
        acc = jnp.dot(xn_ref[...], w_ref[:, hg * hw:(hg + 1) * hw], preferred_element_type=F32)
        parts = []
        for c in range(hw // LANES):
            x = acc[:, c * LANES:(c + 1) * LANES]
            parts.append((x * cos + pltpu.roll(x, LANES // 2, 1) * sin).astype(o_ref.dtype))
        y = jnp.concatenate(parts, axis=-1)
        if lead == 1:
            sub = ps // d
            for s in range(tm // ps):
                for r in range(d):
                    o_ref[hg, 0, r, s * sub:(s + 1) * sub] = y[s * ps + r * sub:s * ps + (r + 1) * sub]
        else:
            for t in range(lead):
                for r in range(d):
                    lo = t * (d * per) + r * per
                    o_ref[hg, t, r] = y[lo:lo + per]


def dilated_in_proj(x, g, w, cos_t, sin_t, *, group, seq, tm=1024):
    n, dm = x.shape
    d = DIL_PAIRS[group][1]
    width = DIL_HEADS * HEAD_DIM
    pt = DIL_QBLK * d
    hw = width // DIL_HGRP
    nsb = seq // tm
    in_specs = [
        pl.BlockSpec((tm, dm), lambda i, k: (i, 0)),
        _full_spec((1, dm)),
        pl.BlockSpec((None, dm, width), lambda i, k: (3 * group + k, 0, 0)),
        pl.BlockSpec((None, tm, LANES), lambda i, k: (k, i % nsb, 0)),
        pl.BlockSpec((None, tm, LANES), lambda i, k: (k, i % nsb, 0)),
    ]
    args = [x, g.reshape(1, dm), w, cos_t, sin_t]
    if d > 1:
        ps = min(pt, DIL_PERM)
        assert tm % ps == 0 and (pt >= tm or pt == ps)
        in_specs.append(_full_spec((ps, ps)))
        args.append(_row_permutation(ps, d))
    if pt >= tm:
        sub = pt // tm
        out_spec = pl.BlockSpec((None, DIL_HGRP, 1, d, tm // d, hw),
                                lambda i, k: (k, 0, i // sub, 0, i % sub, 0))
    else:
        lead = tm // pt
        out_spec = pl.BlockSpec((None, DIL_HGRP, lead, d, DIL_QBLK, hw),
                                lambda i, k: (k, 0, i, 0, 0, 0))
    return pl.pallas_call(
        functools.partial(_dilated_in_proj_kernel, d=d),
        out_shape=jax.ShapeDtypeStruct((3, DIL_HGRP, n // pt, d, DIL_QBLK, hw), BF16),
        grid=(n // tm, 3),
        in_specs=in_specs,
        out_specs=out_spec,
        scratch_shapes=[pltpu.VMEM((tm, dm), BF16)],
        compiler_params=_params("parallel", "arbitrary"),
        name=f"dilated_in_proj{group}",
    )(*args)


def _dilated_weight_slabs(w):
    dm = w.shape[0]
    width = DIL_HEADS * HEAD_DIM
    half = HEAD_DIM // 2
    ng = len(DIL_PAIRS)
    w = w.astype(BF16).reshape(dm, ng, 3, width)
    qk = w[:, :, :2].reshape(dm, ng, 2, width // LANES, 2, 2, half)
    qk = qk.transpose(0, 1, 2, 3, 5, 4, 6).reshape(dm, ng, 2, width)
    w = jnp.concatenate([qk, w[:, :, 2:]], axis=2)
    return w.reshape(dm, 3 * ng, width).transpose(1, 0, 2)


def _dilated_rope_tables(seq, q_scale, d, tm):
    half = HEAD_DIM // 2
    inv = 1.0 / (ROPE_THETA ** (jnp.arange(0, HEAD_DIM, 2, dtype=F32) / HEAD_DIM))
    ang = jnp.arange(seq, dtype=F32)[:, None] * inv[None, :]
    lane = np.arange(LANES)
    cos = jnp.cos(ang)[:, lane % half]
    sin = jnp.sin(ang)[:, lane % half] * jnp.asarray(np.where(lane < LANES // 2, -1.0, 1.0), F32)
    cos = jnp.stack([cos * q_scale, cos, jnp.ones_like(cos)])
    sin = jnp.stack([sin * q_scale, sin, jnp.zeros_like(sin)])

    def tile_order(t):
        t = t.reshape(3, seq // tm, tm // d, d, LANES).transpose(0, 1, 3, 2, 4)
        return t.reshape(3, seq, LANES)

    return tile_order(cos), tile_order(sin)


def _dilated_kernel(*refs, seq):
    ng = len(DIL_PAIRS)
    ins = [refs[7 * g:7 * g + 7] for g in range(ng)]
    out_ref, acc_scr, m_scr, l_scr = refs[7 * ng:]
    i = pl.program_id(1)
    n_pairs = DIL_HGRP // 2
    left = _lane_iota((DIL_QBLK, LANES)) < HEAD_DIM
    q_first = (_lane_iota((DIL_QBLK, LANES)) % HEAD_DIM) < HEAD_DIM // 2
    keep = [q_first.astype(F32).astype(BF16), jnp.logical_not(q_first).astype(F32).astype(BF16)]
    jq = lax.broadcasted_iota(jnp.int32, (DIL_QBLK, DIL_KBLK), 0)
    jk = lax.broadcasted_iota(jnp.int32, (DIL_QBLK, DIL_KBLK), 1)
    rel = jk - jq
    band = jnp.where((rel >= 0) & (rel <= 2 * DIL_HALF), 0.0, NEG_INF).astype(F32)
    jk_row = lax.broadcasted_iota(jnp.int32, (1, DIL_KBLK), 1)

    for step, g in enumerate(DIL_ORDER):
        d = DIL_PAIRS[g][1]
        q_ref, kp_ref, kc_ref, kn_ref, vp_ref, vc_ref, vn_ref = ins[g]
        nt = kc_ref.shape[0]
        sub_len = seq // d
        first = step == 0
        last = step == ng - 1

        def window(p_ref, c_ref, n_ref, t, r, nt=nt):
            if nt == 1:
                before, after = p_ref[r], n_ref[r]
            else:
                before = jnp.where(t == 0, p_ref[r],
                                   c_ref[jnp.maximum(t - 1, 0), r, DIL_HALF:, :])
                after = jnp.where(t == nt - 1, n_ref[r],
                                  c_ref[jnp.minimum(t + 1, nt - 1), r, :DIL_HALF, :])
            return jnp.concatenate([before, c_ref[t, r], after], axis=0)

        def load(idx, d=d, sub_len=sub_len, first=first, q_ref=q_ref, kp_ref=kp_ref,
                 kc_ref=kc_ref, kn_ref=kn_ref, vp_ref=vp_ref, vc_ref=vc_ref, vn_ref=vn_ref,
                 window=window):
            t = idx // d
            r = idx % d
            base = i * (DIL_TOK // d) + t * DIL_QBLK - DIL_HALF
            edge = jnp.where((jk_row >= -base) & (jk_row < sub_len - base), 0.0, NEG_INF)
            start = t * (DIL_QBLK * d) + r
            rows = pl.ds(start, DIL_QBLK, stride=d) if d > 1 else pl.ds(start, DIL_QBLK)
            state = None
            if not first:
                state = [(acc_scr[hp, rows, :], m_scr[hp, rows, :], l_scr[hp, rows, :])
                         for hp in range(n_pairs)]
            return dict(q=q_ref[t, r], k=window(kp_ref, kc_ref, kn_ref, t, r),
                        v=window(vp_ref, vc_ref, vn_ref, t, r),
                        bias=band + edge.astype(F32), rows=rows, state=state)

        def compute(u, first=first, last=last):
            res = []
            for hp in range(n_pairs):
                ps = slice(hp * LANES, (hp + 1) * LANES)
                q, k, v = u["q"][:, ps], u["k"][:, ps], u["v"][:, ps]
                if not first:
                    acc_old, m_old, l_old = u["state"][hp]
                v_ones = jnp.concatenate([v, jnp.ones_like(v)], axis=-1)
                m_h, pv = [], []
                for hh in range(2):
                    s = lax.dot_general(q * keep[hh], k, (((1,), (1,)), ((), ())),
                                        preferred_element_type=F32) + u["bias"]
                    if first:
                        m = jnp.max(s, axis=-1, keepdims=True)
                    else:
                        mine = left if hh == 0 else jnp.logical_not(left)
                        m = jnp.max(jnp.concatenate([s, jnp.where(mine, m_old, NEG_INF)], axis=-1),
                                    axis=-1, keepdims=True)
                    p = jnp.exp2(s - m)
                    m_h.append(m)
                    pv.append(jnp.dot(p.astype(BF16), v_ones, preferred_element_type=F32))
                m_new = jnp.where(left, m_h[0], m_h[1])
                l_new = jnp.where(left, pv[0][:, LANES:], pv[1][:, LANES:])
                acc = jnp.where(left, pv[0][:, :LANES], pv[1][:, :LANES])
                if not first:
                    alpha = jnp.exp2(m_old - m_new)
                    l_new = l_new + alpha * l_old
                    acc = acc + alpha * acc_old
                if last:
                    acc = acc / l_new
                res.append((acc, m_new, l_new))
            return res

        def store(u, res, last=last):
            rows = u["rows"]
            for hp, (acc, m_new, l_new) in enumerate(res):
                if last:
                    out_ref[rows, hp * LANES:(hp + 1) * LANES] = acc.astype(out_ref.dtype)
                else:
                    acc_scr[hp, rows, :] = acc
                    m_scr[hp, rows, :] = m_new
                    l_scr[hp, rows, :] = l_new

        def several_units(j, carry, load=load, compute=compute, store=store):
            units = [load(DIL_UNROLL * j + n) for n in range(DIL_UNROLL)]
            results = [compute(u) for u in units]
            for u, res in zip(units, results):
                store(u, res)
            return carry

        lax.fori_loop(0, nt * d // DIL_UNROLL, several_units, 0)


def dilated_attention(hq, *, batch, seq):
    assert seq % DIL_TOK == 0
    n = batch * seq
    hw = DIL_HEADS * HEAD_DIM // DIL_HGRP
    nblk = seq // DIL_TOK
    in_specs, args = [], []
    for g, (_, d) in enumerate(DIL_PAIRS):
        pt = DIL_QBLK * d
        nt = DIL_TOK // pt
        tiles = seq // pt

        def cur(kind, nt=nt, d=d):
            return pl.BlockSpec((None, None, nt, d, DIL_QBLK, hw),
                                lambda b, i, hg: (kind, hg, b * nblk + i, 0, 0, 0))

        def prev(kind, nt=nt, d=d, tiles=tiles):
            return pl.BlockSpec(
                (None, None, None, d, DIL_HALF, hw),
                lambda b, i, hg: (kind, hg, b * tiles + jnp.maximum(i * nt - 1, 0), 0, 1, 0))

        def nxt(kind, nt=nt, d=d, tiles=tiles):
            return pl.BlockSpec(
                (None, None, None, d, DIL_HALF, hw),
                lambda b, i, hg: (kind, hg, b * tiles + jnp.minimum((i + 1) * nt, tiles - 1), 0, 0, 0))

        in_specs += [cur(0), prev(1), cur(1), nxt(1), prev(2), cur(2), nxt(2)]
        args += [hq[g]] * 7
    return pl.pallas_call(
        functools.partial(_dilated_kernel, seq=seq),
        out_shape=jax.ShapeDtypeStruct((DIL_HGRP, n, hw), BF16),
        grid=(batch, nblk, DIL_HGRP),
        in_specs=in_specs,
        out_specs=pl.BlockSpec((None, DIL_TOK, hw), lambda b, i, hg: (hg, b * nblk + i, 0)),
        scratch_shapes=[pltpu.VMEM((DIL_HGRP // 2, DIL_TOK, LANES), F32)] * 3,
        compiler_params=_params("parallel", "parallel", "arbitrary"),
        name="dilated_attention",
    )(*args)


def _proj_residual_kernel(*refs, n_in):
    a_refs = refs[:n_in]
    w_ref, x_ref, o_ref = refs[n_in:]
    pieces = [a[s] for a in a_refs for s in range(a.shape[0])]
    a = jnp.concatenate(pieces, axis=-1)
    o_ref[...] = x_ref[...] + jnp.dot(a, w_ref[...], preferred_element_type=F32)


def proj_residual(parts, w, x, *, tm=512):
    n, d = x.shape
    in_specs = ([pl.BlockSpec((a.shape[0], tm, a.shape[2]), lambda i: (0, i, 0)) for a in parts]
                + [_full_spec(w.shape), pl.BlockSpec((tm, d), lambda i: (i, 0))])
    return pl.pallas_call(
        functools.partial(_proj_residual_kernel, n_in=len(parts)),
        out_shape=jax.ShapeDtypeStruct((n, d), F32),
        grid=(n // tm,),
        in_specs=in_specs,
        out_specs=pl.BlockSpec((tm, d), lambda i: (i, 0)),
        compiler_params=_params("parallel"),
        name="proj_residual",
    )(*parts, w, x)


def _mlp_kernel(x_ref, g_ref, w1_ref, w2_ref, gf_ref, o_ref, xn_ref, acc_ref, *, final_norm):
    j = pl.program_id(1)

    @pl.when(j == 0)
    def _():
        xn_ref[...] = _rms(x_ref[...], g_ref[...]).astype(BF16)
        acc_ref[...] = jnp.zeros_like(acc_ref)

    h = jnp.dot(xn_ref[...], w1_ref[...], preferred_element_type=F32)
    h = jnp.square(jnp.maximum(h, 0.0)).astype(BF16)
    acc_ref[...] += jnp.dot(h, w2_ref[...], preferred_element_type=F32)

    @pl.when(j == pl.num_programs(1) - 1)
    def _():
        y = x_ref[...] + acc_ref[...]
        if final_norm:
            y = _rms(y, gf_ref[...])
        o_ref[...] = y


def mlp(x, g, w1, w2, g_final, *, final_norm, tm=1024):
    n, d = x.shape
    nf, _, tf = w1.shape
    return pl.pallas_call(
        functools.partial(_mlp_kernel, final_norm=final_norm),
        out_shape=jax.ShapeDtypeStruct((n, d), F32),
        grid=(n // tm, nf),
        in_specs=[
            pl.BlockSpec((tm, d), lambda i, j: (i, 0)),
            _full_spec((1, d)),
            pl.BlockSpec((None, d, tf), lambda i, j: (j, 0, 0)),
            pl.BlockSpec((tf, d), lambda i, j: (j, 0)),
            _full_spec((1, d)),
        ],
        out_specs=pl.BlockSpec((tm, d), lambda i, j: (i, 0)),
        scratch_shapes=[pltpu.VMEM((tm, d), BF16), pltpu.VMEM((tm, d), F32)],
        compiler_params=_params("parallel", "arbitrary"),
        name="mlp",
    )(x, g.reshape(1, d), w1, w2, g_final.reshape(1, d))


def _column_slabs(w, tf):
    d, f = w.shape
    return w.astype(BF16).reshape(d, f // tf, tf).transpose(1, 0, 2)


def _rope_lane_tables(seq, dim, lane_lo, lane_hi, period):
    half = dim // 2
    inv = 1.0 / (ROPE_THETA ** (jnp.arange(0, dim, 2, dtype=F32) / dim))
    ang = jnp.arange(seq, dtype=F32)[:, None] * inv[None, :]
    cos, sin = jnp.cos(ang), jnp.sin(ang)
    lane = np.arange(LANES)
    pos = lane % period
    active = (pos >= lane_lo) & (pos < lane_hi)
    idx = (pos - lane_lo) % half
    second = ((pos - lane_lo) % dim) >= half
    cos_t = jnp.where(active[None, :], cos[:, idx], 1.0)
    sin_t = jnp.where(active[None, :], jnp.where(second[None, :], sin[:, idx], -sin[:, idx]), 0.0)
    return cos_t.astype(F32), sin_t.astype(F32)


def kernel(x, norm_mix, norm_mlp, norm_final, ev_w_in, ev_rpb, ev_q_norm, ev_w_uq,
           ev_kv_norm, ev_w_ukv, ev_w_o, od_w_in, od_w_o, mlp_w1, mlp_w2):
    batch, seq, d = x.shape
    n = batch * seq
    depth = norm_mix.shape[0]
    rows = seq // GRID_W
    width = DIL_HEADS * HEAD_DIM

    dil_q_scale = HEAD_DIM ** -0.5 * LOG2E
    dil_tables = [_dilated_rope_tables(seq, dil_q_scale, d, min(DIL_QBLK * d, DIL_PERM))
                  for _, d in DIL_PAIRS]
    cos_r, sin_r = _rope_lane_tables(seq, MLA_ROPE, MLA_NOPE, MLA_NOPE + MLA_ROPE, MLA_PAD)
    mla_q_scale = (MLA_NOPE + MLA_ROPE) ** -0.5 * LOG2E
    even_width = 3 * NA_WIDTH + MLA_IN

    xs = x.reshape(n, d)
    for layer in range(depth):
        idx = layer // 2
        if layer % 2 == 0:
            w_in = jnp.pad(ev_w_in[idx], ((0, 0), (0, even_width - ev_w_in.shape[2]))).astype(BF16)
            na_qkv, latent = even_in_proj(xs, norm_mix[layer], w_in)
            bias = _na_bias_table(ev_rpb[idx], rows)
            out_a = na_attention(na_qkv, bias, batch=batch, seq=seq)
            q, k, v = mla_prep(latent, ev_q_norm[idx], ev_w_uq[idx], ev_kv_norm[idx],
                               ev_w_ukv[idx], cos_r, sin_r, seq=seq, q_scale=mla_q_scale)
            out_b = flash_attention(q, k, v, batch=batch, seq=seq)
            xs = proj_residual([out_a, out_b], ev_w_o[idx].astype(BF16), xs)
        else:
            w_in = _dilated_weight_slabs(od_w_in[idx])
            hq = [dilated_in_proj(xs, norm_mix[layer], w_in, *dil_tables[g], group=g, seq=seq)
                  for g in range(len(DIL_PAIRS))]
            out = dilated_attention(hq, batch=batch, seq=seq)
            xs = proj_residual([out], od_w_o[idx].astype(BF16), xs)
        xs = mlp(xs, norm_mlp[layer], _column_slabs(mlp_w1[layer], 1024),
                 mlp_w2[layer].astype(BF16), norm_final, final_norm=(layer == depth - 1))
    return xs.reshape(batch, seq, d)
```

```python
import functools
import math

import numpy as np
import jax
import jax.numpy as jnp
from jax import lax
from jax.experimental import pallas as pl
from jax.experimental.pallas import tpu as pltpu

F32 = jnp.float32
BF16 = jnp.bfloat16

D_MODEL = 1024
HEAD_DIM = 64
ROPE_THETA = 10000.0
RMS_EPS = 1e-6
NEG_INF = -1e30
LOG2E = math.log2(math.e)

GRID_W = 64
NA_HEADS = 8
NA_ROWS = 8
NA_COLS = 16
NA_WIDTH = NA_HEADS * HEAD_DIM
NA_UNROLL = 16

MLA_HEADS = 8
MLA_Q_RANK = 256
MLA_KV_RANK = 128
MLA_NOPE = 64
MLA_ROPE = 32
MLA_V = 64
MLA_PAD = 128
MLA_IN = 512

DIL_PAIRS = ((128, 1), (512, 4), (2048, 16))
DIL_HEADS = D_MODEL // HEAD_DIM
DIL_HALF = 64
DIL_QBLK = 128
DIL_KBLK = 256
DIL_HGRP = 4
DIL_TOK = DIL_QBLK * max(d for _, d in DIL_PAIRS)
DIL_ORDER = (2, 1, 0)
DIL_UNROLL = 8
DIL_PERM = 512

D_FF = 4 * D_MODEL

LANES = 128
VMEM_LIMIT_BYTES = 56 * 1024 * 1024


def _params(*semantics):
    return pltpu.CompilerParams(dimension_semantics=semantics,
                                vmem_limit_bytes=VMEM_LIMIT_BYTES)


def _rms(x, g):
    ms = jnp.mean(x * x, axis=-1, keepdims=True)
    return x * lax.rsqrt(ms + RMS_EPS) * g


def _lane_iota(shape):
    return lax.broadcasted_iota(jnp.int32, shape, len(shape) - 1)


def _rope_chunk(x, cos, sin_signed, half):
    lane = _lane_iota(x.shape)
    fwd = pltpu.roll(x, LANES - half, 1)
    bwd = pltpu.roll(x, half, 1)
    rot = jnp.where((lane % (2 * half)) < half, fwd, bwd)
    return x * cos + rot * sin_signed


def _full_spec(shape):
    return pl.BlockSpec(shape, lambda *_: (0,) * len(shape))


def _even_in_proj_kernel(x_ref, g_ref, w_ref, na_ref, h_ref):
    xn = _rms(x_ref[...], g_ref[...]).astype(BF16)
    acc = jnp.dot(xn, w_ref[...], preferred_element_type=F32)
    n_na = na_ref.shape[0]
    q_slabs = n_na // 3
    for c in range(n_na):
        y = acc[:, c * LANES:(c + 1) * LANES]
        if c < q_slabs:
            y = y * (HEAD_DIM ** -0.5 * LOG2E)
        na_ref[c] = y.astype(na_ref.dtype)
    h_ref[...] = acc[:, n_na * LANES:].astype(h_ref.dtype)


def even_in_proj(x, g, w, *, tm=512):
    n, d = x.shape
    f = w.shape[1]
    n_na = 3 * NA_WIDTH // LANES
    assert f == n_na * LANES + MLA_IN
    return pl.pallas_call(
        _even_in_proj_kernel,
        out_shape=(jax.ShapeDtypeStruct((n_na, n, LANES), BF16),
                   jax.ShapeDtypeStruct((n, MLA_IN), BF16)),
        grid=(n // tm,),
        in_specs=[pl.BlockSpec((tm, d), lambda i: (i, 0)),
                  _full_spec((1, d)), _full_spec((d, f))],
        out_specs=(pl.BlockSpec((n_na, tm, LANES), lambda i: (0, i, 0)),
                   pl.BlockSpec((tm, MLA_IN), lambda i: (i, 0))),
        compiler_params=_params("parallel"),
        name="even_in_proj",
    )(x, g.reshape(1, d), w)


def _na_kernel(q_ref, k_ref, v_ref, b_ref, o_ref, *, rows):
    wr = NA_ROWS
    kw = wr * GRID_W
    left = _lane_iota((GRID_W, LANES)) < HEAD_DIM
    keep = [left.astype(F32).astype(BF16), jnp.logical_not(left).astype(F32).astype(BF16)]

    def one_row(r):
        r_start = jnp.clip(r - wr // 2, 0, rows - wr)
        didx = r - r_start
        qoff = pl.multiple_of(r * GRID_W, GRID_W)
        koff = pl.multiple_of(r_start * GRID_W, GRID_W)
        q = q_ref[pl.ds(qoff, GRID_W), :]
        k = k_ref[pl.ds(koff, kw), :]
        v = v_ref[pl.ds(koff, kw), :]
        v_ones = jnp.concatenate([v, jnp.ones_like(v)], axis=-1)
        outs = []
        for hh in range(2):
            s = lax.dot_general(q * keep[hh], k, (((1,), (1,)), ((), ())),
                                preferred_element_type=F32)
            s = s + b_ref[hh, didx]
            m = jnp.max(s, axis=-1, keepdims=True)
            p = jnp.exp2(s - m)
            pv = jnp.dot(p.astype(BF16), v_ones, preferred_element_type=F32)
            outs.append(pv[:, :LANES] / pv[:, LANES:])
        o_ref[pl.ds(qoff, GRID_W), :] = jnp.where(left, outs[0], outs[1]).astype(o_ref.dtype)

    def several_rows(j, carry):
        for n in range(NA_UNROLL):
            one_row(NA_UNROLL * j + n)
        return carry

    lax.fori_loop(0, rows // NA_UNROLL, several_rows, 0)


def _na_bias_table(rpb, rows):
    assert rows >= NA_ROWS
    cols = np.arange(GRID_W)
    col_start = np.clip(cols - NA_COLS // 2, 0, GRID_W - NA_COLS)
    kc = np.arange(GRID_W)
    inside = (kc[None, :] >= col_start[:, None]) & (kc[None, :] < col_start[:, None] + NA_COLS)
    col_rel = kc[None, :] - cols[:, None] + (NA_COLS - 1)
    sel = (col_rel[:, :, None] == np.arange(2 * NA_COLS - 1)[None, None, :]) & inside[:, :, None]
    delta = np.arange(NA_ROWS)
    i = np.arange(NA_ROWS)
    row_rel = i[None, :] - delta[:, None] + (NA_ROWS - 1)
    by_row = rpb.astype(F32)[:, row_rel]
    b = jnp.einsum('hdiv,ckv->hdcik', by_row, jnp.asarray(sel, F32),
                   precision=lax.Precision.HIGHEST)
    b = b + jnp.asarray(np.where(inside, 0.0, NEG_INF), F32)[None, None, :, None, :]
    return (b * LOG2E).reshape(rpb.shape[0], NA_ROWS, GRID_W, NA_ROWS * GRID_W)


def na_attention(qkv, bias, *, batch, seq):
    n = qkv.shape[1]
    rows = seq // GRID_W
    pairs = NA_HEADS // 2
    blk = (None, seq, LANES)
    return pl.pallas_call(
        functools.partial(_na_kernel, rows=rows),
        out_shape=jax.ShapeDtypeStruct((pairs, n, LANES), BF16),
        grid=(batch, pairs),
        in_specs=[
            pl.BlockSpec(blk, lambda b, p: (p, b, 0)),
            pl.BlockSpec(blk, lambda b, p: (pairs + p, b, 0)),
            pl.BlockSpec(blk, lambda b, p: (2 * pairs + p, b, 0)),
            pl.BlockSpec((2, NA_ROWS, GRID_W, NA_ROWS * GRID_W), lambda b, p: (p, 0, 0, 0)),
        ],
        out_specs=pl.BlockSpec(blk, lambda b, p: (p, b, 0)),
        compiler_params=_params("parallel", "parallel"),
        name="na_attention",
    )(qkv, qkv, qkv, bias)


def _mla_prep_kernel(h_ref, qn_ref, kvn_ref, wq_ref, wk_ref, wv_ref, e_ref,
                     cos_ref, sin_ref, q_ref, k_ref, v_ref, *, q_scale):
    h = h_ref[...].astype(F32)
    o1 = MLA_Q_RANK
    o2 = o1 + MLA_KV_RANK
    c_q = _rms(h[:, :o1], qn_ref[...]).astype(BF16)
    c_kv = _rms(h[:, o1:o2], kvn_ref[...]).astype(BF16)
    k_pe = h_ref[:, o2:o2 + MLA_ROPE]
    q = jnp.dot(c_q, wq_ref[...], preferred_element_type=F32)
    k = (jnp.dot(c_kv, wk_ref[...], preferred_element_type=F32)
         + jnp.dot(k_pe, e_ref[...], preferred_element_type=F32))
    v = jnp.dot(c_kv, wv_ref[...], preferred_element_type=F32)
    cos = cos_ref[...]
    sin = sin_ref[...]
    for p in range(MLA_HEADS // 2):
        qs, ks = [], []
        for hd in (2 * p, 2 * p + 1):
            sl = slice(hd * MLA_PAD, (hd + 1) * MLA_PAD)
            qs.append(_rope_chunk(q[:, sl], cos, sin, MLA_ROPE // 2) * q_scale)
            ks.append(_rope_chunk(k[:, sl], cos, sin, MLA_ROPE // 2))
        q_ref[p] = jnp.concatenate(qs, axis=-1).astype(q_ref.dtype)
        k_ref[p] = jnp.concatenate(ks, axis=-1).astype(k_ref.dtype)
        v_pair = v[:, 2 * p * MLA_V:(2 * p + 2) * MLA_V]
        v_ref[p] = jnp.concatenate([v_pair, jnp.ones_like(v_pair)], axis=-1).astype(v_ref.dtype)


def mla_prep(h, q_norm, w_uq, kv_norm, w_ukv, cos_t, sin_t, *, seq, q_scale, tm=512):
    n = h.shape[0]
    hq = MLA_NOPE + MLA_ROPE
    pairs = MLA_HEADS // 2
    wq = w_uq.reshape(MLA_Q_RANK, MLA_HEADS, hq)
    wq = jnp.pad(wq, ((0, 0), (0, 0), (0, MLA_PAD - hq))).reshape(MLA_Q_RANK, MLA_HEADS * MLA_PAD)
    wkv = w_ukv.reshape(MLA_KV_RANK, MLA_HEADS, MLA_NOPE + MLA_V)
    wk = jnp.pad(wkv[:, :, :MLA_NOPE], ((0, 0), (0, 0), (0, MLA_PAD - MLA_NOPE)))
    wk = wk.reshape(MLA_KV_RANK, MLA_HEADS * MLA_PAD)
    wv = wkv[:, :, MLA_NOPE:].reshape(MLA_KV_RANK, MLA_HEADS * MLA_V)
    e = np.zeros((MLA_ROPE, MLA_HEADS, MLA_PAD), np.float32)
    for t in range(MLA_ROPE):
        e[t, :, MLA_NOPE + t] = 1.0
    e = jnp.asarray(e.reshape(MLA_ROPE, MLA_HEADS * MLA_PAD), BF16)
    nsb = seq // tm
    return pl.pallas_call(
        functools.partial(_mla_prep_kernel, q_scale=q_scale),
        out_shape=(jax.ShapeDtypeStruct((pairs, n, 2 * MLA_PAD), BF16),
                   jax.ShapeDtypeStruct((pairs, n, 2 * MLA_PAD), BF16),
                   jax.ShapeDtypeStruct((pairs, n, 2 * MLA_V + LANES), BF16)),
        grid=(n // tm,),
        in_specs=[
            pl.BlockSpec((tm, MLA_IN), lambda i: (i, 0)),
            _full_spec((1, MLA_Q_RANK)),
            _full_spec((1, MLA_KV_RANK)),
            _full_spec(wq.shape), _full_spec(wk.shape), _full_spec(wv.shape), _full_spec(e.shape),
            pl.BlockSpec((tm, LANES), lambda i: (i % nsb, 0)),
            pl.BlockSpec((tm, LANES), lambda i: (i % nsb, 0)),
        ],
        out_specs=(pl.BlockSpec((pairs, tm, 2 * MLA_PAD), lambda i: (0, i, 0)),
                   pl.BlockSpec((pairs, tm, 2 * MLA_PAD), lambda i: (0, i, 0)),
                   pl.BlockSpec((pairs, tm, 2 * MLA_V + LANES), lambda i: (0, i, 0))),
        compiler_params=_params("parallel"),
        name="mla_prep",
    )(h, q_norm.reshape(1, -1), kv_norm.reshape(1, -1),
      wq.astype(BF16), wk.astype(BF16), wv.astype(BF16), e, cos_t, sin_t)


def _flash_kernel(q_ref, k_ref, v_ref, o_ref, m_scr, l_scr, acc_scr, s_scr, p_scr, a_scr, *, chunks):
    tq = q_ref.shape[0]
    nk = len(chunks)
    slots = s_scr.shape[0]
    left = _lane_iota((tq, LANES)) < MLA_V
    m_scr[...] = jnp.full((tq, LANES), NEG_INF, F32)
    l_scr[...] = jnp.zeros((tq, LANES), F32)
    acc_scr[...] = jnp.zeros((tq, LANES), F32)

    def scores(ki):
        off, size = chunks[ki]
        for hh in range(2):
            q = q_ref[:, hh * MLA_PAD:(hh + 1) * MLA_PAD]
            k = k_ref[off:off + size, hh * MLA_PAD:(hh + 1) * MLA_PAD]
            s_scr[ki % slots, hh, :, :size] = lax.dot_general(q, k, (((1,), (1,)), ((), ())),
                                                          preferred_element_type=F32)

    def softmax(ki):
        size = chunks[ki][1]
        m_old = m_scr[...]
        m_new = []
        for hh in range(2):
            s = s_scr[ki % slots, hh, :, :size]
            mine = left if hh == 0 else jnp.logical_not(left)
            m_h = jnp.max(jnp.concatenate([s, jnp.where(mine, m_old, NEG_INF)], axis=-1),
                          axis=-1, keepdims=True)
            p_scr[ki % slots, hh, :, :size] = jnp.exp2(s - m_h).astype(BF16)
            m_new.append(m_h)
        m_pair = jnp.where(left, m_new[0], m_new[1])
        a_scr[ki % slots] = jnp.exp2(m_old - m_pair)
        m_scr[...] = m_pair

    def values(ki):
        off, size = chunks[ki]
        v = v_ref[off:off + size, :]
        pv = [jnp.dot(p_scr[ki % slots, hh, :, :size], v, preferred_element_type=F32) for hh in range(2)]
        alpha = a_scr[ki % slots]
        l_scr[...] = alpha * l_scr[...] + jnp.where(left, pv[0][:, LANES:], pv[1][:, LANES:])
        acc_scr[...] = alpha * acc_scr[...] + jnp.where(left, pv[0][:, :LANES], pv[1][:, :LANES])

    scores(0)
    if nk > 1:
        scores(1)
    softmax(0)
    for ki in range(nk):
        if ki + 2 < nk:
            scores(ki + 2)
        if ki + 1 < nk:
            softmax(ki + 1)
        values(ki)
    o_ref[...] = (acc_scr[...] / l_scr[...]).astype(o_ref.dtype)


def _key_chunks(seq, tk):
    assert seq % tk == 0 and tk % LANES == 0
    return tuple((o, tk) for o in range(0, seq, tk))


def flash_attention(q, k, v, *, batch, seq, tq=256, tk=4096):
    pairs, n, _ = q.shape
    dv = MLA_V
    tq = min(tq, seq)
    tk = min(tk, seq)
    nq = seq // tq
    slots = min(2, seq // tk)
    return pl.pallas_call(
        functools.partial(_flash_kernel, chunks=_key_chunks(seq, tk)),
        out_shape=jax.ShapeDtypeStruct((pairs, n, 2 * dv), BF16),
        grid=(batch, pairs, nq),
        in_specs=[
            pl.BlockSpec((None, tq, 2 * MLA_PAD), lambda b, p, i: (p, b * nq + i, 0)),
            pl.BlockSpec((None, seq, 2 * MLA_PAD), lambda b, p, i: (p, b, 0)),
            pl.BlockSpec((None, seq, 2 * dv + LANES), lambda b, p, i: (p, b, 0)),
        ],
        out_specs=pl.BlockSpec((None, tq, 2 * dv), lambda b, p, i: (p, b * nq + i, 0)),
        scratch_shapes=([pltpu.VMEM((tq, LANES), F32)] * 3
                        + [pltpu.VMEM((slots, 2, tq, tk), F32), pltpu.VMEM((slots, 2, tq, tk), BF16),
                           pltpu.VMEM((slots, tq, LANES), F32)]),
        compiler_params=_params("parallel", "parallel", "arbitrary"),
        name="flash_attention",
    )(q, k, v)


def _row_permutation(tm, d):
    p = np.zeros((tm, tm), np.float32)
    rho = np.arange(tm)
    r, a = rho // (tm // d), rho % (tm // d)
    p[rho, a * d + r] = 1.0
    return jnp.asarray(p, BF16)


def _dilated_in_proj_kernel(*refs, d):
    if d > 1:
        x_ref, g_ref, w_ref, cos_ref, sin_ref, perm_ref, o_ref, xn_ref = refs
    else:
        x_ref, g_ref, w_ref, cos_ref, sin_ref, o_ref, xn_ref = refs
    tm = x_ref.shape[0]

    ps = perm_ref.shape[0] if d > 1 else tm

    @pl.when(pl.program_id(1) == 0)
    def _():
        xn = _rms(x_ref[...], g_ref[...]).astype(BF16)
        if d > 1:
            xn = jnp.concatenate(
                [jnp.dot(perm_ref[...], xn[s * ps:(s + 1) * ps], preferred_element_type=F32)
                 for s in range(tm // ps)], axis=0).astype(BF16)
        xn_ref[...] = xn

    cos = cos_ref[...]
    sin = sin_ref[...]
    lead, _, per = o_ref.shape[1:4]
    hw = o_ref.shape[-1]
    for hg in range(DIL_HGRP):
        acc = jnp.dot(xn_ref[...], w_ref[:, hg * hw:(hg + 1) * hw], preferred_element_type=F32)
        parts = []
        for c in range(hw // LANES):
            x = acc[:, c * LANES:(c + 1) * LANES]
            parts.append((x * cos + pltpu.roll(x, LANES // 2, 1) * sin).astype(o_ref.dtype))
        y = jnp.concatenate(parts, axis=-1)
        if lead == 1:
            sub = ps // d
            for s in range(tm // ps):
                for r in range(d):
                    o_ref[hg, 0, r, s * sub:(s + 1) * sub] = y[s * ps + r * sub:s * ps + (r + 1) * sub]
        else:
            for t in range(lead):
                for r in range(d):
                    lo = t * (d * per) + r * per
                    o_ref[hg, t, r] = y[lo:lo + per]


def dilated_in_proj(x, g, w, cos_t, sin_t, *, group, seq, tm=1024):
    n, dm = x.shape
    d = DIL_PAIRS[group][1]
    width = DIL_HEADS * HEAD_DIM
    pt = DIL_QBLK * d
    hw = width // DIL_HGRP
    nsb = seq // tm
    in_specs = [
        pl.BlockSpec((tm, dm), lambda i, k: (i, 0)),
        _full_spec((1, dm)),
        pl.BlockSpec((None, dm, width), lambda i, k: (3 * group + k, 0, 0)),
        pl.BlockSpec((None, tm, LANES), lambda i, k: (k, i % nsb, 0)),
        pl.BlockSpec((None, tm, LANES), lambda i, k: (k, i % nsb, 0)),
    ]
    args = [x, g.reshape(1, dm), w, cos_t, sin_t]
    if d > 1:
        ps = min(pt, DIL_PERM)
        assert tm % ps == 0 and (pt >= tm or pt == ps)
        in_specs.append(_full_spec((ps, ps)))
        args.append(_row_permutation(ps, d))
    if pt >= tm:
        sub = pt // tm
        out_spec = pl.BlockSpec((None, DIL_HGRP, 1, d, tm // d, hw),
                                lambda i, k: (k, 0, i // sub, 0, i % sub, 0))
    else:
        lead = tm // pt
        out_spec = pl.BlockSpec((None, DIL_HGRP, lead, d, DIL_QBLK, hw),
                                lambda i, k: (k, 0, i, 0, 0, 0))
    return pl.pallas_call(
        functools.partial(_dilated_in_proj_kernel, d=d),
        out_shape=jax.ShapeDtypeStruct((3, DIL_HGRP, n // pt, d, DIL_QBLK, hw), BF16),
        grid=(n // tm, 3),
        in_specs=in_specs,
        out_specs=out_spec,
        scratch_shapes=[pltpu.VMEM((tm, dm), BF16)],
        compiler_params=_params("parallel", "arbitrary"),
        name=f"dilated_in_proj{group}",
    )(*args)


def _dilated_weight_slabs(w):
    dm = w.shape[0]
    width = DIL_HEADS * HEAD_DIM
    half = HEAD_DIM // 2
    ng = len(DIL_PAIRS)
    w = w.astype(BF16).reshape(dm, ng, 3, width)
    qk = w[:, :, :2].reshape(dm, ng, 2, width // LANES, 2, 2, half)
    qk = qk.transpose(0, 1, 2, 3, 5, 4, 6).reshape(dm, ng, 2, width)
    w = jnp.concatenate([qk, w[:, :, 2:]], axis=2)
    return w.reshape(dm, 3 * ng, width).transpose(1, 0, 2)


def _dilated_rope_tables(seq, q_scale, d, tm):
    half = HEAD_DIM // 2
    inv = 1.0 / (ROPE_THETA ** (jnp.arange(0, HEAD_DIM, 2, dtype=F32) / HEAD_DIM))
    ang = jnp.arange(seq, dtype=F32)[:, None] * inv[None, :]
    lane = np.arange(LANES)
    cos = jnp.cos(ang)[:, lane % half]
    sin = jnp.sin(ang)[:, lane % half] * jnp.asarray(np.where(lane < LANES // 2, -1.0, 1.0), F32)
    cos = jnp.stack([cos * q_scale, cos, jnp.ones_like(cos)])
    sin = jnp.stack([sin * q_scale, sin, jnp.zeros_like(sin)])

    def tile_order(t):
        t = t.reshape(3, seq // tm, tm // d, d, LANES).transpose(0, 1, 3, 2, 4)
        return t.reshape(3, seq, LANES)

    return tile_order(cos), tile_order(sin)


def _dilated_kernel(*refs, seq):
    ng = len(DIL_PAIRS)
    ins = [refs[7 * g:7 * g + 7] for g in range(ng)]
    out_ref, acc_scr, m_scr, l_scr = refs[7 * ng:]
    i = pl.program_id(1)
    n_pairs = DIL_HGRP // 2
    left = _lane_iota((DIL_QBLK, LANES)) < HEAD_DIM
    q_first = (_lane_iota((DIL_QBLK, LANES)) % HEAD_DIM) < HEAD_DIM // 2
    keep = [q_first.astype(F32).astype(BF16), jnp.logical_not(q_first).astype(F32).astype(BF16)]
    jq = lax.broadcasted_iota(jnp.int32, (DIL_QBLK, DIL_KBLK), 0)
    jk = lax.broadcasted_iota(jnp.int32, (DIL_QBLK, DIL_KBLK), 1)
    rel = jk - jq
    band = jnp.where((rel >= 0) & (rel <= 2 * DIL_HALF), 0.0, NEG_INF).astype(F32)
    jk_row = lax.broadcasted_iota(jnp.int32, (1, DIL_KBLK), 1)

    for step, g in enumerate(DIL_ORDER):
        d = DIL_PAIRS[g][1]
        q_ref, kp_ref, kc_ref, kn_ref, vp_ref, vc_ref, vn_ref = ins[g]
        nt = kc_ref.shape[0]
        sub_len = seq // d
        first = step == 0
        last = step == ng - 1

        def window(p_ref, c_ref, n_ref, t, r, nt=nt):
            if nt == 1:
                before, after = p_ref[r], n_ref[r]
            else:
                before = jnp.where(t == 0, p_ref[r],
                                   c_ref[jnp.maximum(t - 1, 0), r, DIL_HALF:, :])
                after = jnp.where(t == nt - 1, n_ref[r],
                                  c_ref[jnp.minimum(t + 1, nt - 1), r, :DIL_HALF, :])
            return jnp.concatenate([before, c_ref[t, r], after], axis=0)

        def load(idx, d=d, sub_len=sub_len, first=first, q_ref=q_ref, kp_ref=kp_ref,
                 kc_ref=kc_ref, kn_ref=kn_ref, vp_ref=vp_ref, vc_ref=vc_ref, vn_ref=vn_ref,
                 window=window):
            t = idx // d
            r = idx % d
            base = i * (DIL_TOK // d) + t * DIL_QBLK - DIL_HALF
            edge = jnp.where((jk_row >= -base) & (jk_row < sub_len - base), 0.0, NEG_INF)
            start = t * (DIL_QBLK * d) + r
            rows = pl.ds(start, DIL_QBLK, stride=d) if d > 1 else pl.ds(start, DIL_QBLK)
            state = None
            if not first:
                state = [(acc_scr[hp, rows, :], m_scr[hp, rows, :], l_scr[hp, rows, :])
                         for hp in range(n_pairs)]
            return dict(q=q_ref[t, r], k=window(kp_ref, kc_ref, kn_ref, t, r),
                        v=window(vp_ref, vc_ref, vn_ref, t, r),
                        bias=band + edge.astype(F32), rows=rows, state=state)

        def compute(u, first=first, last=last):
            res = []
            for hp in range(n_pairs):
                ps = slice(hp * LANES, (hp + 1) * LANES)
                q, k, v = u["q"][:, ps], u["k"][:, ps], u["v"][:, ps]
                if not first:
                    acc_old, m_old, l_old = u["state"][hp]
                v_ones = jnp.concatenate([v, jnp.ones_like(v)], axis=-1)
                m_h, pv = [], []
                for hh in range(2):
                    s = lax.dot_general(q * keep[hh], k, (((1,), (1,)), ((), ())),
                                        preferred_element_type=F32) + u["bias"]
                    if first:
                        m = jnp.max(s, axis=-1, keepdims=True)
                    else:
                        mine = left if hh == 0 else jnp.logical_not(left)
                        m = jnp.max(jnp.concatenate([s, jnp.where(mine, m_old, NEG_INF)], axis=-1),
                                    axis=-1, keepdims=True)
                    p = jnp.exp2(s - m)
                    m_h.append(m)
                    pv.append(jnp.dot(p.astype(BF16), v_ones, preferred_element_type=F32))
                m_new = jnp.where(left, m_h[0], m_h[1])
                l_new = jnp.where(left, pv[0][:, LANES:], pv[1][:, LANES:])
                acc = jnp.where(left, pv[0][:, :LANES], pv[1][:, :LANES])
                if not first:
                    alpha = jnp.exp2(m_old - m_new)
                    l_new = l_new + alpha * l_old
                    acc = acc + alpha * acc_old
                if last:
                    acc = acc / l_new
                res.append((acc, m_new, l_new))
            return res

        def store(u, res, last=last):
            rows = u["rows"]
            for hp, (acc, m_new, l_new) in enumerate(res):
                if last:
                    out_ref[rows, hp * LANES:(hp + 1) * LANES] = acc.astype(out_ref.dtype)
                else:
                    acc_scr[hp, rows, :] = acc
                    m_scr[hp, rows, :] = m_new
                    l_scr[hp, rows, :] = l_new

        def several_units(j, carry, load=load, compute=compute, store=store):
            units = [load(DIL_UNROLL * j + n) for n in range(DIL_UNROLL)]
            results = [compute(u) for u in units]
            for u, res in zip(units, results):
                store(u, res)
            return carry

        lax.fori_loop(0, nt * d // DIL_UNROLL, several_units, 0)


def dilated_attention(hq, *, batch, seq):
    assert seq % DIL_TOK == 0
    n = batch * seq
    hw = DIL_HEADS * HEAD_DIM // DIL_HGRP
    nblk = seq // DIL_TOK
    in_specs, args = [], []
    for g, (_, d) in enumerate(DIL_PAIRS):
        pt = DIL_QBLK * d
        nt = DIL_TOK // pt
        tiles = seq // pt

        def cur(kind, nt=nt, d=d):
            return pl.BlockSpec((None, None, nt, d, DIL_QBLK, hw),
                                lambda b, i, hg: (kind, hg, b * nblk + i, 0, 0, 0))

        def prev(kind, nt=nt, d=d, tiles=tiles):
            return pl.BlockSpec(
                (None, None, None, d, DIL_HALF, hw),
                lambda b, i, hg: (kind, hg, b * tiles + jnp.maximum(i * nt - 1, 0), 0, 1, 0))

        def nxt(kind, nt=nt, d=d, tiles=tiles):
            return pl.BlockSpec(
                (None, None, None, d, DIL_HALF, hw),
                lambda b, i, hg: (kind, hg, b * tiles + jnp.minimum((i + 1) * nt, tiles - 1), 0, 0, 0))

        in_specs += [cur(0), prev(1), cur(1), nxt(1), prev(2), cur(2), nxt(2)]
        args += [hq[g]] * 7
    return pl.pallas_call(
        functools.partial(_dilated_kernel, seq=seq),
        out_shape=jax.ShapeDtypeStruct((DIL_HGRP, n, hw), BF16),
        grid=(batch, nblk, DIL_HGRP),
        in_specs=in_specs,
        out_specs=pl.BlockSpec((None, DIL_TOK, hw), lambda b, i, hg: (hg, b * nblk + i, 0)),
        scratch_shapes=[pltpu.VMEM((DIL_HGRP // 2, DIL_TOK, LANES), F32)] * 3,
        compiler_params=_params("parallel", "parallel", "arbitrary"),
        name="dilated_attention",
    )(*args)


def _proj_residual_kernel(*refs, n_in):
    a_refs = refs[:n_in]
    w_ref, x_ref, o_ref = refs[n_in:]
    pieces = [a[s] for a in a_refs for s in range(a.shape[0])]
    a = jnp.concatenate(pieces, axis=-1)
    o_ref[...] = x_ref[...] + jnp.dot(a, w_ref[...], preferred_element_type=F32)


def proj_residual(parts, w, x, *, tm=512):
    n, d = x.shape
    in_specs = ([pl.BlockSpec((a.shape[0], tm, a.shape[2]), lambda i: (0, i, 0)) for a in parts]
                + [_full_spec(w.shape), pl.BlockSpec((tm, d), lambda i: (i, 0))])
    return pl.pallas_call(
        functools.partial(_proj_residual_kernel, n_in=len(parts)),
        out_shape=jax.ShapeDtypeStruct((n, d), F32),
        grid=(n // tm,),
        in_specs=in_specs,
        out_specs=pl.BlockSpec((tm, d), lambda i: (i, 0)),
        compiler_params=_params("parallel"),
        name="proj_residual",
    )(*parts, w, x)


def _mlp_kernel(x_ref, g_ref, w1_ref, w2_ref, gf_ref, o_ref, xn_ref, acc_ref, *, final_norm):
    j = pl.program_id(1)

    @pl.when(j == 0)
    def _():
        xn_ref[...] = _rms(x_ref[...], g_ref[...]).astype(BF16)
        acc_ref[...] = jnp.zeros_like(acc_ref)

    h = jnp.dot(xn_ref[...], w1_ref[...], preferred_element_type=F32)
    h = jnp.square(jnp.maximum(h, 0.0)).astype(BF16)
    acc_ref[...] += jnp.dot(h, w2_ref[...], preferred_element_type=F32)

    @pl.when(j == pl.num_programs(1) - 1)
    def _():
        y = x_ref[...] + acc_ref[...]
        if final_norm:
            y = _rms(y, gf_ref[...])
        o_ref[...] = y


def mlp(x, g, w1, w2, g_final, *, final_norm, tm=1024):
    n, d = x.shape
    nf, _, tf = w1.shape
    return pl.pallas_call(
        functools.partial(_mlp_kernel, final_norm=final_norm),
        out_shape=jax.ShapeDtypeStruct((n, d), F32),
        grid=(n // tm, nf),
        in_specs=[
            pl.BlockSpec((tm, d), lambda i, j: (i, 0)),
            _full_spec((1, d)),
            pl.BlockSpec((None, d, tf), lambda i, j: (j, 0, 0)),
            pl.BlockSpec((tf, d), lambda i, j: (j, 0)),
            _full_spec((1, d)),
        ],
        out_specs=pl.BlockSpec((tm, d), lambda i, j: (i, 0)),
        scratch_shapes=[pltpu.VMEM((tm, d), BF16), pltpu.VMEM((tm, d), F32)],
        compiler_params=_params("parallel", "arbitrary"),
        name="mlp",
    )(x, g.reshape(1, d), w1, w2, g_final.reshape(1, d))


def _column_slabs(w, tf):
    d, f = w.shape
    return w.astype(BF16).reshape(d, f // tf, tf).transpose(1, 0, 2)


def _rope_lane_tables(seq, dim, lane_lo, lane_hi, period):
    half = dim // 2
    inv = 1.0 / (ROPE_THETA ** (jnp.arange(0, dim, 2, dtype=F32) / dim))
    ang = jnp.arange(seq, dtype=F32)[:, None] * inv[None, :]
    cos, sin = jnp.cos(ang), jnp.sin(ang)
    lane = np.arange(LANES)
    pos = lane % period
    active = (pos >= lane_lo) & (pos < lane_hi)
    idx = (pos - lane_lo) % half
    second = ((pos - lane_lo) % dim) >= half
    cos_t = jnp.where(active[None, :], cos[:, idx], 1.0)
    sin_t = jnp.where(active[None, :], jnp.where(second[None, :], sin[:, idx], -sin[:, idx]), 0.0)
    return cos_t.astype(F32), sin_t.astype(F32)


def kernel(x, norm_mix, norm_mlp, norm_final, ev_w_in, ev_rpb, ev_q_norm, ev_w_uq,
           ev_kv_norm, ev_w_ukv, ev_w_o, od_w_in, od_w_o, mlp_w1, mlp_w2):
    batch, seq, d = x.shape
    n = batch * seq
    depth = norm_mix.shape[0]
    rows = seq // GRID_W
    width = DIL_HEADS * HEAD_DIM

    dil_q_scale = HEAD_DIM ** -0.5 * LOG2E
    dil_tables = [_dilated_rope_tables(seq, dil_q_scale, d, min(DIL_QBLK * d, DIL_PERM))
                  for _, d in DIL_PAIRS]
    cos_r, sin_r = _rope_lane_tables(seq, MLA_ROPE, MLA_NOPE, MLA_NOPE + MLA_ROPE, MLA_PAD)
    mla_q_scale = (MLA_NOPE + MLA_ROPE) ** -0.5 * LOG2E
    even_width = 3 * NA_WIDTH + MLA_IN

    xs = x.reshape(n, d)
    for layer in range(depth):
        idx = layer // 2
        if layer % 2 == 0:
            w_in = jnp.pad(ev_w_in[idx], ((0, 0), (0, even_width - ev_w_in.shape[2]))).astype(BF16)
            na_qkv, latent = even_in_proj(xs, norm_mix[layer], w_in)
            bias = _na_bias_table(ev_rpb[idx], rows)
            out_a = na_attention(na_qkv, bias, batch=batch, seq=seq)
            q, k, v = mla_prep(latent, ev_q_norm[idx], ev_w_uq[idx], ev_kv_norm[idx],
                               ev_w_ukv[idx], cos_r, sin_r, seq=seq, q_scale=mla_q_scale)
            out_b = flash_attention(q, k, v, batch=batch, seq=seq)
            xs = proj_residual([out_a, out_b], ev_w_o[idx].astype(BF16), xs)
        else:
            w_in = _dilated_weight_slabs(od_w_in[idx])
            hq = [dilated_in_proj(xs, norm_mix[layer], w_in, *dil_tables[g], group=g, seq=seq)
                  for g in range(len(DIL_PAIRS))]
            out = dilated_attention(hq, batch=batch, seq=seq)
            xs = proj_residual([out], od_w_o[idx].astype(BF16), xs)
        xs = mlp(xs, norm_mlp[layer], _column_slabs(mlp_w1[layer], 1024),
                 mlp_w2[layer].astype(BF16), norm_final, final_norm=(layer == depth - 1))
    return xs.reshape(batch, seq, d)
```

```python
import functools
import math

import numpy as np
import jax
import jax.numpy as jnp
from jax import lax
from jax.experimental import pallas as pl
from jax.experimental.pallas import tpu as pltpu

F32 = jnp.float32
BF16 = jnp.bfloat16

D_MODEL = 1024
HEAD_DIM = 64
ROPE_THETA = 10000.0
RMS_EPS = 1e-6
NEG_INF = -1e30
LOG2E = math.log2(math.e)

GRID_W = 64
NA_HEADS = 8
NA_ROWS = 8
NA_COLS = 16
NA_WIDTH = NA_HEADS * HEAD_DIM
NA_UNROLL = 32

MLA_HEADS = 8
MLA_Q_RANK = 256
MLA_KV_RANK = 128
MLA_NOPE = 64
MLA_ROPE = 32
MLA_V = 64
MLA_PAD = 128
MLA_IN = 512
_MLA_LANE_OF = np.concatenate([
    np.arange(MLA_NOPE // 2), MLA_PAD // 2 + np.arange(MLA_NOPE // 2),
    MLA_NOPE // 2 + np.arange(MLA_ROPE // 2), MLA_PAD // 2 + MLA_NOPE // 2 + np.arange(MLA_ROPE // 2)])

DIL_PAIRS = ((128, 1), (512, 4), (2048, 16))
DIL_HEADS = D_MODEL // HEAD_DIM
DIL_HALF = 64
DIL_QBLK = 128
DIL_KBLK = 256
DIL_HGRP = 4
DIL_TOK = DIL_QBLK * max(d for _, d in DIL_PAIRS)
DIL_ORDER = (2, 1, 0)
DIL_UNROLL = 16
DIL_PERM = 512

D_FF = 4 * D_MODEL

LANES = 128
VMEM_LIMIT_BYTES = 56 * 1024 * 1024


def _params(*semantics):
    return pltpu.CompilerParams(dimension_semantics=semantics,
                                vmem_limit_bytes=VMEM_LIMIT_BYTES)


def _rms(x, g):
    ms = jnp.mean(x * x, axis=-1, keepdims=True)
    return x * lax.rsqrt(ms + RMS_EPS) * g


def _lane_iota(shape):
    return lax.broadcasted_iota(jnp.int32, shape, len(shape) - 1)


def _rope_chunk(x, cos, sin_signed):
    return x * cos + pltpu.roll(x, LANES // 2, 1) * sin_signed


def _full_spec(shape):
    return pl.BlockSpec(shape, lambda *_: (0,) * len(shape))


def _even_in_proj_kernel(x_ref, g_ref, w_ref, na_ref, h_ref):
    xn = _rms(x_ref[...], g_ref[...]).astype(BF16)
    acc = jnp.dot(xn, w_ref[...], preferred_element_type=F32)
    n_na = na_ref.shape[0]
    q_slabs = n_na // 3
    for c in range(n_na):
        y = acc[:, c * LANES:(c + 1) * LANES]
        if c < q_slabs:
            y = y * (HEAD_DIM ** -0.5 * LOG2E)
        na_ref[c] = y.astype(na_ref.dtype)
    h_ref[...] = acc[:, n_na * LANES:].astype(h_ref.dtype)


def even_in_proj(x, g, w, *, tm=512):
    n, d = x.shape
    f = w.shape[1]
    n_na = 3 * NA_WIDTH // LANES
    assert f == n_na * LANES + MLA_IN
    return pl.pallas_call(
        _even_in_proj_kernel,
        out_shape=(jax.ShapeDtypeStruct((n_na, n, LANES), BF16),
                   jax.ShapeDtypeStruct((n, MLA_IN), BF16)),
        grid=(n // tm,),
        in_specs=[pl.BlockSpec((tm, d), lambda i: (i, 0)),
                  _full_spec((1, d)), _full_spec((d, f))],
        out_specs=(pl.BlockSpec((n_na, tm, LANES), lambda i: (0, i, 0)),
                   pl.BlockSpec((tm, MLA_IN), lambda i: (i, 0))),
        compiler_params=_params("parallel"),
        name="even_in_proj",
    )(x, g.reshape(1, d), w)


def _na_kernel(q_ref, k_ref, v_ref, b_ref, o_ref, *, rows):
    wr = NA_ROWS
    kw = wr * GRID_W
    left = _lane_iota((GRID_W, LANES)) < HEAD_DIM
    keep = [left.astype(F32).astype(BF16), jnp.logical_not(left).astype(F32).astype(BF16)]

    def one_row(r):
        r_start = jnp.clip(r - wr // 2, 0, rows - wr)
        didx = r - r_start
        qoff = pl.multiple_of(r * GRID_W, GRID_W)
        koff = pl.multiple_of(r_start * GRID_W, GRID_W)
        q = q_ref[pl.ds(qoff, GRID_W), :]
        k = k_ref[pl.ds(koff, kw), :]
        v = v_ref[pl.ds(koff, kw), :]
        v_ones = jnp.concatenate([v, jnp.ones_like(v)], axis=-1)
        outs = []
        for hh in range(2):
            s = lax.dot_general(q * keep[hh], k, (((1,), (1,)), ((), ())),
                                preferred_element_type=F32)
            s = s + b_ref[hh, didx]
            m = jnp.max(s, axis=-1, keepdims=True)
            p = jnp.exp2(s - m)
            pv = jnp.dot(p.astype(BF16), v_ones, preferred_element_type=F32)
            outs.append(pv[:, :LANES] / pv[:, LANES:])
        o_ref[pl.ds(qoff, GRID_W), :] = jnp.where(left, outs[0], outs[1]).astype(o_ref.dtype)

    def several_rows(j, carry):
        for n in range(NA_UNROLL):
            one_row(NA_UNROLL * j + n)
        return carry

    lax.fori_loop(0, rows // NA_UNROLL, several_rows, 0)


def _na_bias_table(rpb, rows):
    assert rows >= NA_ROWS
    cols = np.arange(GRID_W)
    col_start = np.clip(cols - NA_COLS // 2, 0, GRID_W - NA_COLS)
    kc = np.arange(GRID_W)
    inside = (kc[None, :] >= col_start[:, None]) & (kc[None, :] < col_start[:, None] + NA_COLS)
    col_rel = kc[None, :] - cols[:, None] + (NA_COLS - 1)
    sel = (col_rel[:, :, None] == np.arange(2 * NA_COLS - 1)[None, None, :]) & inside[:, :, None]
    delta = np.arange(NA_ROWS)
    i = np.arange(NA_ROWS)
    row_rel = i[None, :] - delta[:, None] + (NA_ROWS - 1)
    by_row = rpb.astype(F32)[:, row_rel]
    b = jnp.einsum('hdiv,ckv->hdcik', by_row, jnp.asarray(sel, F32),
                   precision=lax.Precision.HIGHEST)
    b = b + jnp.asarray(np.where(inside, 0.0, NEG_INF), F32)[None, None, :, None, :]
    return (b * LOG2E).reshape(rpb.shape[0], NA_ROWS, GRID_W, NA_ROWS * GRID_W)


def na_attention(qkv, bias, *, batch, seq):
    n = qkv.shape[1]
    rows = seq // GRID_W
    pairs = NA_HEADS // 2
    blk = (None, seq, LANES)
    return pl.pallas_call(
        functools.partial(_na_kernel, rows=rows),
        out_shape=jax.ShapeDtypeStruct((pairs, n, LANES), BF16),
        grid=(batch, pairs),
        in_specs=[
            pl.BlockSpec(blk, lambda b, p: (p, b, 0)),
            pl.BlockSpec(blk, lambda b, p: (pairs + p, b, 0)),
            pl.BlockSpec(blk, lambda b, p: (2 * pairs + p, b, 0)),
            pl.BlockSpec((2, NA_ROWS, GRID_W, NA_ROWS * GRID_W), lambda b, p: (p, 0, 0, 0)),
        ],
        out_specs=pl.BlockSpec(blk, lambda b, p: (p, b, 0)),
        compiler_params=_params("parallel", "parallel"),
        name="na_attention",
    )(qkv, qkv, qkv, bias)


def _mla_prep_kernel(h_ref, qn_ref, kvn_ref, wq_ref, wk_ref, wv_ref, e_ref,
                     cos_ref, sin_ref, q_ref, k_ref, v_ref, *, q_scale):
    h = h_ref[...].astype(F32)
    o1 = MLA_Q_RANK
    o2 = o1 + MLA_KV_RANK
    c_q = _rms(h[:, :o1], qn_ref[...]).astype(BF16)
    c_kv = _rms(h[:, o1:o2], kvn_ref[...]).astype(BF16)
    k_pe = h_ref[:, o2:o2 + MLA_ROPE]
    q = jnp.dot(c_q, wq_ref[...], preferred_element_type=F32)
    k = (jnp.dot(c_kv, wk_ref[...], preferred_element_type=F32)
         + jnp.dot(k_pe, e_ref[...], preferred_element_type=F32))
    v = jnp.dot(c_kv, wv_ref[...], preferred_element_type=F32)
    cos = cos_ref[...]
    sin = sin_ref[...]
    for p in range(MLA_HEADS // 2):
        qs, ks = [], []
        for hd in (2 * p, 2 * p + 1):
            sl = slice(hd * MLA_PAD, (hd + 1) * MLA_PAD)
            qs.append(_rope_chunk(q[:, sl], cos, sin) * q_scale)
            ks.append(_rope_chunk(k[:, sl], cos, sin))
        q_ref[p] = jnp.concatenate(qs, axis=-1).astype(q_ref.dtype)
        k_ref[p] = jnp.concatenate(ks, axis=-1).astype(k_ref.dtype)
        v_pair = v[:, 2 * p * MLA_V:(2 * p + 2) * MLA_V]
        v_ref[p] = jnp.concatenate([v_pair, jnp.ones_like(v_pair)], axis=-1).astype(v_ref.dtype)


def mla_prep(h, q_norm, w_uq, kv_norm, w_ukv, cos_t, sin_t, *, seq, q_scale, tm=512):
    n = h.shape[0]
    hq = MLA_NOPE + MLA_ROPE
    pairs = MLA_HEADS // 2
    wq = jnp.zeros((MLA_Q_RANK, MLA_HEADS, MLA_PAD), w_uq.dtype)
    wq = wq.at[:, :, _MLA_LANE_OF].set(w_uq.reshape(MLA_Q_RANK, MLA_HEADS, hq))
    wq = wq.reshape(MLA_Q_RANK, MLA_HEADS * MLA_PAD)
    wkv = w_ukv.reshape(MLA_KV_RANK, MLA_HEADS, MLA_NOPE + MLA_V)
    wk = jnp.zeros((MLA_KV_RANK, MLA_HEADS, MLA_PAD), w_ukv.dtype)
    wk = wk.at[:, :, _MLA_LANE_OF[:MLA_NOPE]].set(wkv[:, :, :MLA_NOPE])
    wk = wk.reshape(MLA_KV_RANK, MLA_HEADS * MLA_PAD)
    wv = wkv[:, :, MLA_NOPE:].reshape(MLA_KV_RANK, MLA_HEADS * MLA_V)
    e = np.zeros((MLA_ROPE, MLA_HEADS, MLA_PAD), np.float32)
    for t in range(MLA_ROPE):
        e[t, :, _MLA_LANE_OF[MLA_NOPE + t]] = 1.0
    e = jnp.asarray(e.reshape(MLA_ROPE, MLA_HEADS * MLA_PAD), BF16)
    nsb = seq // tm
    return pl.pallas_call(
        functools.partial(_mla_prep_kernel, q_scale=q_scale),
        out_shape=(jax.ShapeDtypeStruct((pairs, n, 2 * MLA_PAD), BF16),
                   jax.ShapeDtypeStruct((pairs, n, 2 * MLA_PAD), BF16),
                   jax.ShapeDtypeStruct((pairs, n, 2 * MLA_V + LANES), BF16)),
        grid=(n // tm,),
        in_specs=[
            pl.BlockSpec((tm, MLA_IN), lambda i: (i, 0)),
            _full_spec((1, MLA_Q_RANK)),
            _full_spec((1, MLA_KV_RANK)),
            _full_spec(wq.shape), _full_spec(wk.shape), _full_spec(wv.shape), _full_spec(e.shape),
            pl.BlockSpec((tm, LANES), lambda i: (i % nsb, 0)),
            pl.BlockSpec((tm, LANES), lambda i: (i % nsb, 0)),
        ],
        out_specs=(pl.BlockSpec((pairs, tm, 2 * MLA_PAD), lambda i: (0, i, 0)),
                   pl.BlockSpec((pairs, tm, 2 * MLA_PAD), lambda i: (0, i, 0)),
                   pl.BlockSpec((pairs, tm, 2 * MLA_V + LANES), lambda i: (0, i, 0))),
        compiler_params=_params("parallel"),
        name="mla_prep",
    )(h, q_norm.reshape(1, -1), kv_norm.reshape(1, -1),
      wq.astype(BF16), wk.astype(BF16), wv.astype(BF16), e, cos_t, sin_t)


def _flash_kernel(q_ref, k_ref, v_ref, o_ref, m_scr, l_scr, acc_scr, s_scr, p_scr, a_scr, *, chunks):
    tq = q_ref.shape[0]
    nk = len(chunks)
    slots = s_scr.shape[0]
    left = _lane_iota((tq, LANES)) < MLA_V
    m_scr[...] = jnp.full((tq, LANES), NEG_INF, F32)
    l_scr[...] = jnp.zeros((tq, LANES), F32)
    acc_scr[...] = jnp.zeros((tq, LANES), F32)

    def scores(ki):
        off, size = chunks[ki]
        for hh in range(2):
            q = q_ref[:, hh * MLA_PAD:(hh + 1) * MLA_PAD]
            k = k_ref[off:off + size, hh * MLA_PAD:(hh + 1) * MLA_PAD]
            s_scr[ki % slots, hh, :, :size] = lax.dot_general(q, k, (((1,), (1,)), ((), ())),
                                                          preferred_element_type=F32)

    def softmax(ki):
        size = chunks[ki][1]
        m_old = m_scr[...]
        m_new = []
        for hh in range(2):
            s = s_scr[ki % slots, hh, :, :size]
            mine = left if hh == 0 else jnp.logical_not(left)
            m_h = jnp.max(jnp.concatenate([s, jnp.where(mine, m_old, NEG_INF)], axis=-1),
                          axis=-1, keepdims=True)
            p_scr[ki % slots, hh, :, :size] = jnp.exp2(s - m_h).astype(BF16)
            m_new.append(m_h)
        m_pair = jnp.where(left, m_new[0], m_new[1])
        a_scr[ki % slots] = jnp.exp2(m_old - m_pair)
        m_scr[...] = m_pair

    def values(ki):
        off, size = chunks[ki]
        v = v_ref[off:off + size, :]
        pv = [jnp.dot(p_scr[ki % slots, hh, :, :size], v, preferred_element_type=F32) for hh in range(2)]
        alpha = a_scr[ki % slots]
        l_scr[...] = alpha * l_scr[...] + jnp.where(left, pv[0][:, LANES:], pv[1][:, LANES:])
        acc_scr[...] = alpha * acc_scr[...] + jnp.where(left, pv[0][:, :LANES], pv[1][:, :LANES])

    scores(0)
    if nk > 1:
        scores(1)
    softmax(0)
    for ki in range(nk):
        if ki + 2 < nk:
            scores(ki + 2)
        if ki + 1 < nk:
            softmax(ki + 1)
        values(ki)
    o_ref[...] = (acc_scr[...] / l_scr[...]).astype(o_ref.dtype)


def _key_chunks(seq, tk):
    assert seq % tk == 0 and tk % LANES == 0
    return tuple((o, tk) for o in range(0, seq, tk))


def flash_attention(q, k, v, *, batch, seq, tq=256, tk=4096):
    pairs, n, _ = q.shape
    dv = MLA_V
    tq = min(tq, seq)
    tk = min(tk, seq)
    nq = seq // tq
    slots = min(2, seq // tk)
    return pl.pallas_call(
        functools.partial(_flash_kernel, chunks=_key_chunks(seq, tk)),
        out_shape=jax.ShapeDtypeStruct((pairs, n, 2 * dv), BF16),
        grid=(batch, pairs, nq),
        in_specs=[
            pl.BlockSpec((None, tq, 2 * MLA_PAD), lambda b, p, i: (p, b * nq + i, 0)),
            pl.BlockSpec((None, seq, 2 * MLA_PAD), lambda b, p, i: (p, b, 0)),
            pl.BlockSpec((None, seq, 2 * dv + LANES), lambda b, p, i: (p, b, 0)),
        ],
        out_specs=pl.BlockSpec((None, tq, 2 * dv), lambda b, p, i: (p, b * nq + i, 0)),
        scratch_shapes=([pltpu.VMEM((tq, LANES), F32)] * 3
                        + [pltpu.VMEM((slots, 2, tq, tk), F32), pltpu.VMEM((slots, 2, tq, tk), BF16),
                           pltpu.VMEM((slots, tq, LANES), F32)]),
        compiler_params=_params("parallel", "parallel", "arbitrary"),
        name="flash_attention",
    )(q, k, v)


def _row_permutation(tm, d):
    p = np.zeros((tm, tm), np.float32)
    rho = np.arange(tm)
    r, a = rho // (tm // d), rho % (tm // d)
    p[rho, a * d + r] = 1.0
    return jnp.asarray(p, BF16)


def _dilated_in_proj_kernel(*refs, d):
    if d > 1:
        x_ref, g_ref, w_ref, cos_ref, sin_ref, perm_ref, o_ref, xn_ref = refs
    else:
        x_ref, g_ref, w_ref, cos_ref, sin_ref, o_ref, xn_ref = refs
    tm = x_ref.shape[0]

    ps = perm_ref.shape[0] if d > 1 else tm

    @pl.when(pl.program_id(1) == 0)
    def _():
        xn = _rms(x_ref[...], g_ref[...]).astype(BF16)
        if d > 1:
            xn = jnp.concatenate(
                [jnp.dot(perm_ref[...], xn[s * ps:(s + 1) * ps], preferred_element_type=F32)
                 for s in range(tm // ps)], axis=0).astype(BF16)
        xn_ref[...] = xn

    cos = cos_ref[...]
    sin = sin_ref[...]
    lead, _, per = o_ref.shape[1:4]
    hw = o_ref.shape[-1]
    for hg in range(DIL_HGRP):
        acc = jnp.dot(xn_ref[...], w_ref[:, hg * hw:(hg + 1) * hw], preferred_element_type=F32)
        parts = []
        for c in range(hw // LANES):
            x = acc[:, c * LANES:(c + 1) * LANES]
            parts.append((x * cos + pltpu.roll(x, LANES // 2, 1) * sin).astype(o_ref.dtype))
        y = jnp.concatenate(parts, axis=-1)
        if lead == 1:
            sub = ps // d
            for s in range(tm // ps):
                for r in range(d):
                    o_ref[hg, 0, r, s * sub:(s + 1) * sub] = y[s * ps + r * sub:s * ps + (r + 1) * sub]
        else:
            for t in range(lead):
                for r in range(d):
                    lo = t * (d * per) + r * per
                    o_ref[hg, t, r] = y[lo:lo + per]


def dilated_in_proj(x, g, w, cos_t, sin_t, *, group, seq, tm=1024):
    n, dm = x.shape
    d = DIL_PAIRS[group][1]
    width = DIL_HEADS * HEAD_DIM
    pt = DIL_QBLK * d
    hw = width // DIL_HGRP
    nsb = seq // tm
    in_specs = [
        pl.BlockSpec((tm, dm), lambda i, k: (i, 0)),
        _full_spec((1, dm)),
        pl.BlockSpec((None, dm, width), lambda i, k: (3 * group + k, 0, 0)),
        pl.BlockSpec((None, tm, LANES), lambda i, k: (k, i % nsb, 0)),
        pl.BlockSpec((None, tm, LANES), lambda i, k: (k, i % nsb, 0)),
    ]
    args = [x, g.reshape(1, dm), w, cos_t, sin_t]
    if d > 1:
        ps = min(pt, DIL_PERM)
        assert tm % ps == 0 and (pt >= tm or pt == ps)
        in_specs.append(_full_spec((ps, ps)))
        args.append(_row_permutation(ps, d))
    if pt >= tm:
        sub = pt // tm
        out_spec = pl.BlockSpec((None, DIL_HGRP, 1, d, tm // d, hw),
                                lambda i, k: (k, 0, i // sub, 0, i % sub, 0))
    else:
        lead = tm // pt
        out_spec = pl.BlockSpec((None, DIL_HGRP, lead, d, DIL_QBLK, hw),
                                lambda i, k: (k, 0, i, 0, 0, 0))
    return pl.pallas_call(
        functools.partial(_dilated_in_proj_kernel, d=d),
        out_shape=jax.ShapeDtypeStruct((3, DIL_HGRP, n // pt, d, DIL_QBLK, hw), BF16),
        grid=(n // tm, 3),
        in_specs=in_specs,
        out_specs=out_spec,
        scratch_shapes=[pltpu.VMEM((tm, dm), BF16)],
        compiler_params=_params("parallel", "arbitrary"),
        name=f"dilated_in_proj{group}",
    )(*args)


def _dilated_weight_slabs(w):
    dm = w.shape[0]
    width = DIL_HEADS * HEAD_DIM
    half = HEAD_DIM // 2
    ng = len(DIL_PAIRS)
    w = w.astype(BF16).reshape(dm, ng, 3, width)
    qk = w[:, :, :2].reshape(dm, ng, 2, width // LANES, 2, 2, half)
    qk = qk.transpose(0, 1, 2, 3, 5, 4, 6).reshape(dm, ng, 2, width)
    w = jnp.concatenate([qk, w[:, :, 2:]], axis=2)
    return w.reshape(dm, 3 * ng, width).transpose(1, 0, 2)


def _dilated_rope_tables(seq, q_scale, d, tm):
    half = HEAD_DIM // 2
    inv = 1.0 / (ROPE_THETA ** (jnp.arange(0, HEAD_DIM, 2, dtype=F32) / HEAD_DIM))
    ang = jnp.arange(seq, dtype=F32)[:, None] * inv[None, :]
    lane = np.arange(LANES)
    cos = jnp.cos(ang)[:, lane % half]
    sin = jnp.sin(ang)[:, lane % half] * jnp.asarray(np.where(lane < LANES // 2, -1.0, 1.0), F32)
    cos = jnp.stack([cos * q_scale, cos, jnp.ones_like(cos)])
    sin = jnp.stack([sin * q_scale, sin, jnp.zeros_like(sin)])

    def tile_order(t):
        t = t.reshape(3, seq // tm, tm // d, d, LANES).transpose(0, 1, 3, 2, 4)
        return t.reshape(3, seq, LANES)

    return tile_order(cos), tile_order(sin)


def _dilated_kernel(*refs, seq):
    ng = len(DIL_PAIRS)
    ins = [refs[7 * g:7 * g + 7] for g in range(ng)]
    out_ref, acc_scr, m_scr, l_scr = refs[7 * ng:]
    i = pl.program_id(1)
    n_pairs = DIL_HGRP // 2
    left = _lane_iota((DIL_QBLK, LANES)) < HEAD_DIM
    q_first = (_lane_iota((DIL_QBLK, LANES)) % HEAD_DIM) < HEAD_DIM // 2
    keep = [q_first.astype(F32).astype(BF16), jnp.logical_not(q_first).astype(F32).astype(BF16)]
    jq = lax.broadcasted_iota(jnp.int32, (DIL_QBLK, DIL_KBLK), 0)
    jk = lax.broadcasted_iota(jnp.int32, (DIL_QBLK, DIL_KBLK), 1)
    rel = jk - jq
    band = jnp.where((rel >= 0) & (rel <= 2 * DIL_HALF), 0.0, NEG_INF).astype(F32)
    jk_row = lax.broadcasted_iota(jnp.int32, (1, DIL_KBLK), 1)

    for step, g in enumerate(DIL_ORDER):
        d = DIL_PAIRS[g][1]
        q_ref, kp_ref, kc_ref, kn_ref, vp_ref, vc_ref, vn_ref = ins[g]
        nt = kc_ref.shape[0]
        sub_len = seq // d
        first = step == 0
        last = step == ng - 1

        def window(p_ref, c_ref, n_ref, t, r, nt=nt):
            if nt == 1:
                before, after = p_ref[r], n_ref[r]
            else:
                before = jnp.where(t == 0, p_ref[r],
                                   c_ref[jnp.maximum(t - 1, 0), r, DIL_HALF:, :])
                after = jnp.where(t == nt - 1, n_ref[r],
                                  c_ref[jnp.minimum(t + 1, nt - 1), r, :DIL_HALF, :])
            return jnp.concatenate([before, c_ref[t, r], after], axis=0)

        def load(idx, d=d, sub_len=sub_len, first=first, q_ref=q_ref, kp_ref=kp_ref,
                 kc_ref=kc_ref, kn_ref=kn_ref, vp_ref=vp_ref, vc_ref=vc_ref, vn_ref=vn_ref,
                 window=window):
            t = idx // d
            r = idx % d
            base = i * (DIL_TOK // d) + t * DIL_QBLK - DIL_HALF
            edge = jnp.where((jk_row >= -base) & (jk_row < sub_len - base), 0.0, NEG_INF)
            start = t * (DIL_QBLK * d) + r
            rows = pl.ds(start, DIL_QBLK, stride=d) if d > 1 else pl.ds(start, DIL_QBLK)
            state = None
            if not first:
                state = [(acc_scr[hp, rows, :], m_scr[hp, rows, :], l_scr[hp, rows, :])
                         for hp in range(n_pairs)]
            return dict(q=q_ref[t, r], k=window(kp_ref, kc_ref, kn_ref, t, r),
                        v=window(vp_ref, vc_ref, vn_ref, t, r),
                        bias=band + edge.astype(F32), rows=rows, state=state)

        def compute(u, first=first, last=last):
            res = []
            for hp in range(n_pairs):
                ps = slice(hp * LANES, (hp + 1) * LANES)
                q, k, v = u["q"][:, ps], u["k"][:, ps], u["v"][:, ps]
                if not first:
                    acc_old, m_old, l_old = u["state"][hp]
                v_ones = jnp.concatenate([v, jnp.ones_like(v)], axis=-1)
                m_h, pv = [], []
                for hh in range(2):
                    s = lax.dot_general(q * keep[hh], k, (((1,), (1,)), ((), ())),
                                        preferred_element_type=F32) + u["bias"]
                    if first:
                        m = jnp.max(s, axis=-1, keepdims=True)
                    else:
                        mine = left if hh == 0 else jnp.logical_not(left)
                        m = jnp.max(jnp.concatenate([s, jnp.where(mine, m_old, NEG_INF)], axis=-1),
                                    axis=-1, keepdims=True)
                    p = jnp.exp2(s - m)
                    m_h.append(m)
                    pv.append(jnp.dot(p.astype(BF16), v_ones, preferred_element_type=F32))
                m_new = jnp.where(left, m_h[0], m_h[1])
                l_new = jnp.where(left, pv[0][:, LANES:], pv[1][:, LANES:])
                acc = jnp.where(left, pv[0][:, :LANES], pv[1][:, :LANES])
                if not first:
                    alpha = jnp.exp2(m_old - m_new)
                    l_new = l_new + alpha * l_old
                    acc = acc + alpha * acc_old
                if last:
                    acc = acc / l_new
                res.append((acc, m_new, l_new))
            return res

        def store(u, res, last=last):
            rows = u["rows"]
            for hp, (acc, m_new, l_new) in enumerate(res):
                if last:
                    out_ref[rows, hp * LANES:(hp + 1) * LANES] = acc.astype(out_ref.dtype)
                else:
                    acc_scr[hp, rows, :] = acc
                    m_scr[hp, rows, :] = m_new
                    l_scr[hp, rows, :] = l_new

        def several_units(j, carry, load=load, compute=compute, store=store):
            units = [load(DIL_UNROLL * j + n) for n in range(DIL_UNROLL)]
            results = [compute(u) for u in units]
            for u, res in zip(units, results):
                store(u, res)
            return carry

        lax.fori_loop(0, nt * d // DIL_UNROLL, several_units, 0)


def dilated_attention(hq, *, batch, seq):
    assert seq % DIL_TOK == 0
    n = batch * seq
    hw = DIL_HEADS * HEAD_DIM // DIL_HGRP
    nblk = seq // DIL_TOK
    in_specs, args = [], []
    for g, (_, d) in enumerate(DIL_PAIRS):
        pt = DIL_QBLK * d
        nt = DIL_TOK // pt
        tiles = seq // pt

        def cur(kind, nt=nt, d=d):
            return pl.BlockSpec((None, None, nt, d, DIL_QBLK, hw),
                                lambda b, i, hg: (kind, hg, b * nblk + i, 0, 0, 0))

        def prev(kind, nt=nt, d=d, tiles=tiles):
            return pl.BlockSpec(
                (None, None, None, d, DIL_HALF, hw),
                lambda b, i, hg: (kind, hg, b * tiles + jnp.maximum(i * nt - 1, 0), 0, 1, 0))

        def nxt(kind, nt=nt, d=d, tiles=tiles):
            return pl.BlockSpec(
                (None, None, None, d, DIL_HALF, hw),
                lambda b, i, hg: (kind, hg, b * tiles + jnp.minimum((i + 1) * nt, tiles - 1), 0, 0, 0))

        in_specs += [cur(0), prev(1), cur(1), nxt(1), prev(2), cur(2), nxt(2)]
        args += [hq[g]] * 7
    return pl.pallas_call(
        functools.partial(_dilated_kernel, seq=seq),
        out_shape=jax.ShapeDtypeStruct((DIL_HGRP, n, hw), BF16),
        grid=(batch, nblk, DIL_HGRP),
        in_specs=in_specs,
        out_specs=pl.BlockSpec((None, DIL_TOK, hw), lambda b, i, hg: (hg, b * nblk + i, 0)),
        scratch_shapes=[pltpu.VMEM((DIL_HGRP // 2, DIL_TOK, LANES), F32)] * 3,
        compiler_params=_params("parallel", "parallel", "arbitrary"),
        name="dilated_attention",
    )(*args)


def _proj_residual_kernel(*refs, n_in):
    a_refs = refs[:n_in]
    w_ref, x_ref, o_ref = refs[n_in:]
    pieces = [a[s] for a in a_refs for s in range(a.shape[0])]
    a = jnp.concatenate(pieces, axis=-1)
    o_ref[...] = x_ref[...] + jnp.dot(a, w_ref[...], preferred_element_type=F32)


def proj_residual(parts, w, x, *, tm=512):
    n, d = x.shape
    in_specs = ([pl.BlockSpec((a.shape[0], tm, a.shape[2]), lambda i: (0, i, 0)) for a in parts]
                + [_full_spec(w.shape), pl.BlockSpec((tm, d), lambda i: (i, 0))])
    return pl.pallas_call(
        functools.partial(_proj_residual_kernel, n_in=len(parts)),
        out_shape=jax.ShapeDtypeStruct((n, d), F32),
        grid=(n // tm,),
        in_specs=in_specs,
        out_specs=pl.BlockSpec((tm, d), lambda i: (i, 0)),
        compiler_params=_params("parallel"),
        name="proj_residual",
    )(*parts, w, x)


def _mlp_kernel(x_ref, g_ref, w1_ref, w2_ref, gf_ref, o_ref, xn_ref, acc_ref, *, final_norm):
    j = pl.program_id(1)

    @pl.when(j == 0)
    def _():
        xn_ref[...] = _rms(x_ref[...], g_ref[...]).astype(BF16)
        acc_ref[...] = jnp.zeros_like(acc_ref)

    h = jnp.dot(xn_ref[...], w1_ref[...], preferred_element_type=F32)
    h = jnp.square(jnp.maximum(h, 0.0)).astype(BF16)
    acc_ref[...] += jnp.dot(h, w2_ref[...], preferred_element_type=F32)

    @pl.when(j == pl.num_programs(1) - 1)
    def _():
        y = x_ref[...] + acc_ref[...]
        if final_norm:
            y = _rms(y, gf_ref[...])
        o_ref[...] = y


def mlp(x, g, w1, w2, g_final, *, final_norm, tm=1024):
    n, d = x.shape
    nf, _, tf = w1.shape
    return pl.pallas_call(
        functools.partial(_mlp_kernel, final_norm=final_norm),
        out_shape=jax.ShapeDtypeStruct((n, d), F32),
        grid=(n // tm, nf),
        in_specs=[
            pl.BlockSpec((tm, d), lambda i, j: (i, 0)),
            _full_spec((1, d)),
            pl.BlockSpec((None, d, tf), lambda i, j: (j, 0, 0)),
            pl.BlockSpec((tf, d), lambda i, j: (j, 0)),
            _full_spec((1, d)),
        ],
        out_specs=pl.BlockSpec((tm, d), lambda i, j: (i, 0)),
        scratch_shapes=[pltpu.VMEM((tm, d), BF16), pltpu.VMEM((tm, d), F32)],
        compiler_params=_params("parallel", "arbitrary"),
        name="mlp",
    )(x, g.reshape(1, d), w1, w2, g_final.reshape(1, d))


def _column_slabs(w, tf):
    d, f = w.shape
    return w.astype(BF16).reshape(d, f // tf, tf).transpose(1, 0, 2)


def _mla_rope_tables(seq):
    half = MLA_ROPE // 2
    inv = 1.0 / (ROPE_THETA ** (jnp.arange(0, MLA_ROPE, 2, dtype=F32) / MLA_ROPE))
    ang = jnp.arange(seq, dtype=F32)[:, None] * inv[None, :]
    cos, sin = jnp.cos(ang), jnp.sin(ang)
    first = _MLA_LANE_OF[MLA_NOPE:MLA_NOPE + half]
    second = _MLA_LANE_OF[MLA_NOPE + half:]
    cos_t = jnp.ones((seq, LANES), F32).at[:, first].set(cos).at[:, second].set(cos)
    sin_t = jnp.zeros((seq, LANES), F32).at[:, first].set(-sin).at[:, second].set(sin)
    return cos_t, sin_t


def kernel(x, norm_mix, norm_mlp, norm_final, ev_w_in, ev_rpb, ev_q_norm, ev_w_uq,
           ev_kv_norm, ev_w_ukv, ev_w_o, od_w_in, od_w_o, mlp_w1, mlp_w2):
    batch, seq, d = x.shape
    n = batch * seq
    depth = norm_mix.shape[0]
    rows = seq // GRID_W
    width = DIL_HEADS * HEAD_DIM

    dil_q_scale = HEAD_DIM ** -0.5 * LOG2E
    dil_tables = [_dilated_rope_tables(seq, dil_q_scale, d, min(DIL_QBLK * d, DIL_PERM))
                  for _, d in DIL_PAIRS]
    cos_r, sin_r = _mla_rope_tables(seq)
    mla_q_scale = (MLA_NOPE + MLA_ROPE) ** -0.5 * LOG2E
    even_width = 3 * NA_WIDTH + MLA_IN

    xs = x.reshape(n, d)
    for layer in range(depth):
        idx = layer // 2
        if layer % 2 == 0:
            w_in = jnp.pad(ev_w_in[idx], ((0, 0), (0, even_width - ev_w_in.shape[2]))).astype(BF16)
            na_qkv, latent = even_in_proj(xs, norm_mix[layer], w_in)
            bias = _na_bias_table(ev_rpb[idx], rows)
            out_a = na_attention(na_qkv, bias, batch=batch, seq=seq)
            q, k, v = mla_prep(latent, ev_q_norm[idx], ev_w_uq[idx], ev_kv_norm[idx],
                               ev_w_ukv[idx], cos_r, sin_r, seq=seq, q_scale=mla_q_scale)
            out_b = flash_attention(q, k, v, batch=batch, seq=seq)
            xs = proj_residual([out_a, out_b], ev_w_o[idx].astype(BF16), xs)
        else:
            w_in = _dilated_weight_slabs(od_w_in[idx])
            hq = [dilated_in_proj(xs, norm_mix[layer], w_in, *dil_tables[g], group=g, seq=seq)
                  for g in range(len(DIL_PAIRS))]
            out = dilated_attention(hq, batch=batch, seq=seq)
            xs = proj_residual([out], od_w_o[idx].astype(BF16), xs)
        xs = mlp(xs, norm_mlp[layer], _column_slabs(mlp_w1[layer], 1024),
                 mlp_w2[layer].astype(BF16), norm_final, final_norm=(layer == depth - 1))
    return xs.reshape(batch, seq, d)
```
